```python
import math
import jax, jax.numpy as jnp
from jax import lax
import numpy as np

D_MODEL = 1024
BATCH = 2
SEQ = 8192
DEPTH = 1
DEC_BATCH = 128
DEC_SEQ = 4
PAST_LEN = 8192
PAGE_SIZE = 128

H_RET = 4
DK_RET = 128
DV_RET = 256
RET_CHUNK = 128
H_DIFF = 8
D_HEAD = 64
DV_DIFF = 2 * D_HEAD
Q_BLOCK = 128
D_FF = -(-(8 * D_MODEL) // (3 * 256)) * 256
ROPE_THETA = 10000.0
EPS = 1e-6

W_QR = H_RET * DK_RET
W_KR = H_RET * DK_RET
W_VR = H_RET * DV_RET
W_GR = H_RET * DV_RET
W_QD = H_DIFF * 2 * D_HEAD
W_KD = H_DIFF * 2 * D_HEAD
W_VD = H_DIFF * DV_DIFF
W_GATES = 2 * D_MODEL
IN_WIDTH = W_QR + W_KR + W_VR + W_GR + W_QD + W_KD + W_VD + W_GATES
SPLITS = list(np.cumsum([W_QR, W_KR, W_VR, W_GR, W_QD, W_KD, W_VD]).tolist())

kernel_name = "retention_diffattn_gated_hybrid_step"


def rmsnorm(x, g):
    x32 = x.astype(jnp.float32)
    y = x32 * lax.rsqrt(jnp.mean(x32 * x32, axis=-1, keepdims=True) + EPS)
    return (y * g.astype(jnp.float32)).astype(x.dtype)


def rope(x, pos):
    d = x.shape[-1]
    half = d // 2
    inv = 1.0 / jnp.power(ROPE_THETA, jnp.arange(half, dtype=jnp.float32) / half)
    ang = pos.astype(jnp.float32)[:, None] * inv[None, :]
    bshape = (pos.shape[0],) + (1,) * (x.ndim - 3) + (half,)
    cos = jnp.cos(ang).reshape(bshape)
    sin = jnp.sin(ang).reshape(bshape)
    x32 = x.astype(jnp.float32)
    x1, x2 = x32[..., :half], x32[..., half:]
    return jnp.concatenate([x1 * cos - x2 * sin, x2 * cos + x1 * sin], axis=-1).astype(x.dtype)


def retention(q, k, v, s0):
    B, T, H, _ = q.shape
    dv = v.shape[-1]
    C = T if T <= RET_CHUNK else RET_CHUNK
    n = T // C
    lg = jnp.log(1.0 - jnp.power(2.0, -5.0 - jnp.arange(H, dtype=jnp.float32)))
    idx = jnp.arange(C, dtype=jnp.float32)
    dist = idx[:, None] - idx[None, :]
    decay_mask = jnp.where(dist[None] >= 0, jnp.exp(jnp.maximum(dist, 0.0)[None] * lg[:, None, None]), 0.0)
    xi = jnp.exp((idx[:, None] + 1.0) * lg[None, :])
    zeta = jnp.exp((C - 1.0 - idx)[:, None] * lg[None, :])
    chunk_decay = jnp.exp(C * lg)

    def to_chunks(a):
        return a.astype(jnp.float32).reshape((B, n, C) + a.shape[2:]).swapaxes(0, 1)

    def step(S, inp):
        qc, kc, vc = inp
        att = jnp.einsum('bihd,bjhd->bhij', qc, kc) * decay_mask[None]
        o = jnp.einsum('bhij,bjhe->bihe', att, vc) + jnp.einsum('bihd,bhde->bihe', qc, S) * xi[None, :, :, None]
        S = chunk_decay[None, :, None, None] * S + jnp.einsum('bjhd,bjhe->bhde', kc * zeta[None, :, :, None], vc)
        return S, o

    S, o = lax.scan(step, s0.astype(jnp.float32), (to_chunks(q), to_chunks(k), to_chunks(v)))
    o = o.swapaxes(0, 1).reshape(B, T, H, dv)
    return o.astype(v.dtype), S.astype(s0.dtype)


def diff_core(q, k, v, q_pos, k_pos, lam):
    s = jnp.einsum('bqhmd,bkhmd->bhmqk', q, k, preferred_element_type=jnp.float32) * (D_HEAD ** -0.5)
    mask = k_pos[None, :] <= q_pos[:, None]
    s = jnp.where(mask[None, None, None], s, -jnp.inf)
    p = jax.nn.softmax(s, axis=-1)
    a = p[:, :, 0] - lam * p[:, :, 1]
    return jnp.einsum('bhqk,bkhe->bqhe', a.astype(v.dtype), v)


def diff_prompt(q, k, v, pos, lam):
    B, T = q.shape[:2]
    QB = min(Q_BLOCK, T)
    nb = T // QB
    qb = q.reshape((B, nb, QB) + q.shape[2:]).swapaxes(0, 1)
    pb = pos.reshape(nb, QB)
    o = lax.map(lambda a: diff_core(a[0], k, v, a[1], pos, lam), (qb, pb))
    return o.swapaxes(0, 1).reshape(B, T, H_DIFF, DV_DIFF)


def diff_sample(q, k, v, cache_k, cache_v, layer, page_table, past_len, lam):
    Ts = q.shape[1]
    q_pos = past_len + jnp.arange(Ts, dtype=jnp.int32)
    k_pos = jnp.arange(past_len + Ts, dtype=jnp.int32)

    def one(a):
        qs, ks, vs, pt = a
        kp = cache_k[layer, pt].reshape(past_len, H_DIFF, 2, D_HEAD).astype(ks.dtype)
        vp = cache_v[layer, pt].reshape(past_len, H_DIFF, DV_DIFF).astype(vs.dtype)
        kk = jnp.concatenate([kp, ks], axis=0)
        vv = jnp.concatenate([vp, vs], axis=0)
        return diff_core(qs[None], kk[None], vv[None], q_pos, k_pos, lam)[0]

    return lax.map(one, (q, k, v, page_table))


def mixer_project(h, pos, w_in, qn_g, kn_g):
    B, T, _ = h.shape
    z = h @ w_in
    q_r, k_r, v_r, g_r, q_d, k_d, v_d, gts = jnp.split(z, SPLITS, axis=-1)
    q_r = rope(q_r.reshape(B, T, H_RET, DK_RET), pos)
    k_r = rope(k_r.reshape(B, T, H_RET, DK_RET), pos) * (DK_RET ** -0.5)
    v_r = v_r.reshape(B, T, H_RET, DV_RET)
    q_d = rope(rmsnorm(q_d.reshape(B, T, H_DIFF, 2, D_HEAD), qn_g), pos)
    k_d = rope(rmsnorm(k_d.reshape(B, T, H_DIFF, 2, D_HEAD), kn_g), pos)
    v_d = v_d.reshape(B, T, H_DIFF, DV_DIFF)
    gts = gts.reshape(B, T, 2, D_MODEL)
    return q_r, k_r, v_r, g_r, q_d, k_d, v_d, gts


def mixer_merge(o_r, g_r, o_d, gts, ret_gn_g, diff_gn_g, lam_init, w_br_ret, w_br_diff, w_out):
    B, T = o_r.shape[:2]
    y_r = rmsnorm(o_r, ret_gn_g).reshape(B, T, H_RET * DV_RET) * jax.nn.silu(g_r)
    y_d = (rmsnorm(o_d, diff_gn_g) * (1.0 - lam_init)).reshape(B, T, H_DIFF * DV_DIFF)
    gate = jax.nn.sigmoid(gts)
    m = gate[:, :, 0] * (y_r @ w_br_ret) + gate[:, :, 1] * (y_d @ w_br_diff)
    return m @ w_out


def swiglu(h, wg, wu, wd):
    return (jax.nn.silu(h @ wg) * (h @ wu)) @ wd


def setup_inputs(seed: int = 0) -> dict:
    key = jax.random.key(seed)
    ks = jax.random.split(key, 24)
    f32 = jnp.float32
    n_pages = PAST_LEN // PAGE_SIZE
    n_used = DEC_BATCH * n_pages
    n_phys = n_used + max(1, n_used // 4)

    def nrm(k, shape, scale):
        return jax.random.normal(k, shape, f32) * scale

    def gain(k, shape):
        return 1.0 + nrm(k, shape, 0.01)

    x_prompt = nrm(ks[0], (BATCH, SEQ, D_MODEL), 1.0)
    x_sample = nrm(ks[1], (DEC_BATCH, DEC_SEQ, D_MODEL), 1.0)
    cache_k = nrm(ks[2], (DEPTH, n_phys, PAGE_SIZE, H_DIFF, 2 * D_HEAD), 1.0)
    cache_v = nrm(ks[3], (DEPTH, n_phys, PAGE_SIZE, H_DIFF, DV_DIFF), 1.0)
    state_ret = nrm(ks[4], (DEPTH, DEC_BATCH, H_RET, DK_RET, DV_RET), 0.05)
    page_table = jax.random.permutation(ks[5], n_phys)[:n_used].reshape(DEC_BATCH, n_pages).astype(jnp.int32)
    return {
        'x_prompt': x_prompt,
        'x_sample': x_sample,
        'cache_k': cache_k,
        'cache_v': cache_v,
        'state_ret': state_ret,
        'page_table': page_table,
        'norm_mix_g': gain(ks[6], (DEPTH, D_MODEL)),
        'w_in': nrm(ks[7], (DEPTH, D_MODEL, IN_WIDTH), D_MODEL ** -0.5),
        'ret_gn_g': gain(ks[8], (DEPTH, H_RET, DV_RET)),
        'diff_qnorm_g': gain(ks[9], (DEPTH, D_HEAD)),
        'diff_knorm_g': gain(ks[10], (DEPTH, D_HEAD)),
        'lambda_q1': nrm(ks[11], (DEPTH, D_HEAD), 0.1),
        'lambda_k1': nrm(ks[12], (DEPTH, D_HEAD), 0.1),
        'lambda_q2': nrm(ks[13], (DEPTH, D_HEAD), 0.1),
        'lambda_k2': nrm(ks[14], (DEPTH, D_HEAD), 0.1),
        'diff_gn_g': gain(ks[15], (DEPTH, DV_DIFF)),
        'w_br_ret': nrm(ks[16], (DEPTH, H_RET * DV_RET, D_MODEL), (H_RET * DV_RET) ** -0.5),
        'w_br_diff': nrm(ks[17], (DEPTH, H_DIFF * DV_DIFF, D_MODEL), (H_DIFF * DV_DIFF) ** -0.5),
        'w_out': nrm(ks[18], (DEPTH, D_MODEL, D_MODEL), D_MODEL ** -0.5),
        'norm_ffn_g': gain(ks[19], (DEPTH, D_MODEL)),
        'w_ffn_gate': nrm(ks[20], (DEPTH, D_MODEL, D_FF), D_MODEL ** -0.5),
        'w_ffn_up': nrm(ks[21], (DEPTH, D_MODEL, D_FF), D_MODEL ** -0.5),
        'w_ffn_down': nrm(ks[22], (DEPTH, D_FF, D_MODEL), D_FF ** -0.5),
    }


def reference(x_prompt, x_sample, cache_k, cache_v, state_ret, page_table, norm_mix_g, w_in, ret_gn_g, diff_qnorm_g, diff_knorm_g, lambda_q1, lambda_k1, lambda_q2, lambda_k2, diff_gn_g, w_br_ret, w_br_diff, w_out, norm_ffn_g, w_ffn_gate, w_ffn_up, w_ffn_down):
    B, T, _ = x_prompt.shape
    Bs, Ts, _ = x_sample.shape
    past_len = page_table.shape[1] * cache_k.shape[2]
    pos_p = jnp.arange(T, dtype=jnp.int32)
    pos_s = past_len + jnp.arange(Ts, dtype=jnp.int32)
    xp, xs = x_prompt, x_sample
    nk_p, nv_p, ns_p, nk_s, nv_s, ns_s = [], [], [], [], [], []
    for l in range(DEPTH):
        lam_init = 0.8 - 0.6 * math.exp(-0.3 * l)
        lam = (jnp.exp(jnp.sum(lambda_q1[l].astype(jnp.float32) * lambda_k1[l].astype(jnp.float32)))
               - jnp.exp(jnp.sum(lambda_q2[l].astype(jnp.float32) * lambda_k2[l].astype(jnp.float32))) + lam_init)

        h = rmsnorm(xp, norm_mix_g[l])
        q_r, k_r, v_r, g_r, q_d, k_d, v_d, gts = mixer_project(h, pos_p, w_in[l], diff_qnorm_g[l], diff_knorm_g[l])
        o_r, s_p = retention(q_r, k_r, v_r, jnp.zeros((B, H_RET, DK_RET, DV_RET), xp.dtype))
        o_d = diff_prompt(q_d, k_d, v_d, pos_p, lam)
        xp = xp + mixer_merge(o_r, g_r, o_d, gts, ret_gn_g[l], diff_gn_g[l], lam_init, w_br_ret[l], w_br_diff[l], w_out[l])
        xp = xp + swiglu(rmsnorm(xp, norm_ffn_g[l]), w_ffn_gate[l], w_ffn_up[l], w_ffn_down[l])
        nk_p.append(k_d.reshape(B, T, H_DIFF, 2 * D_HEAD))
        nv_p.append(v_d)
        ns_p.append(s_p)

        h = rmsnorm(xs, norm_mix_g[l])
        q_r, k_r, v_r, g_r, q_d, k_d, v_d, gts = mixer_project(h, pos_s, w_in[l], diff_qnorm_g[l], diff_knorm_g[l])
        o_r, s_s = retention(q_r, k_r, v_r, state_ret[l])
        o_d = diff_sample(q_d, k_d, v_d, cache_k, cache_v, l, page_table, past_len, lam)
        xs = xs + mixer_merge(o_r, g_r, o_d, gts, ret_gn_g[l], diff_gn_g[l], lam_init, w_br_ret[l], w_br_diff[l], w_out[l])
        xs = xs + swiglu(rmsnorm(xs, norm_ffn_g[l]), w_ffn_gate[l], w_ffn_up[l], w_ffn_down[l])
        nk_s.append(k_d.reshape(Bs, Ts, H_DIFF, 2 * D_HEAD))
        nv_s.append(v_d)
        ns_s.append(s_s)

    new_k_prompt = jnp.stack(nk_p)
    new_v_prompt = jnp.stack(nv_p)
    new_state_ret_prompt = jnp.stack(ns_p)
    new_k_sample = jnp.stack(nk_s)
    new_v_sample = jnp.stack(nv_s)
    new_state_ret_sample = jnp.stack(ns_s)
    return (xp, xs, new_k_prompt, new_v_prompt, new_state_ret_prompt, new_k_sample, new_v_sample, new_state_ret_sample)
```

```python
import functools
import math

import jax
import jax.numpy as jnp
from jax import lax
from jax.experimental import pallas as pl
from jax.experimental.pallas import tpu as pltpu

F32 = jnp.float32
BF16 = jnp.bfloat16

D_MODEL = 1024
H_RET = 4
DK_RET = 128
DV_RET = 256
H_DIFF = 8
D_HEAD = 64
DV_DIFF = 2 * D_HEAD
ROPE_THETA = 10000.0
EPS = 1e-6

LANES = 128
SECTION = 1024
N_SECTIONS = 8
NORM_GROUP_TILE = 256
VMEM_LIMIT = 56 * 1024 * 1024

TOKEN_TILE = 512
RET_CHUNK = 512
ATTN_TILE = 256
PAGES_PER_STEP = 8
RET_SAMPLE_GROUP = 8

NT_DIMS = (((1,), (1,)), ((), ()))
TN_DIMS = (((0,), (0,)), ((), ()))


def _sigmoid(x):
    return 1.0 / (1.0 + jnp.exp(-x))


def _params(semantics):
    return pltpu.CompilerParams(dimension_semantics=semantics, vmem_limit_bytes=VMEM_LIMIT)


def _inproj_kernel(x_ref, g_ref, w_ref, cr_ref, sr_ref, cd_ref, sd_ref, qg_ref, kg_ref, bd_ref,
                   z_ref, k32_ref, v32_ref, h_ref):
    j = pl.program_id(1)

    @pl.when(j == 0)
    def _():
        x = x_ref[...]
        ms = jnp.mean(x * x, axis=-1, keepdims=True)
        h_ref[...] = (x * lax.rsqrt(ms + EPS) * g_ref[...]).astype(BF16)

    acc = jnp.dot(h_ref[...], w_ref[...], preferred_element_type=F32)

    @pl.when(j == 0)
    def _():
        c = cr_ref[...]
        s = sr_ref[...]
        for hh in range(SECTION // LANES):
            blk = acc[:, hh * LANES:(hh + 1) * LANES]
            r = blk * c + pltpu.roll(blk, DK_RET // 2, 1) * s
            if hh >= H_RET:
                r = r * (DK_RET ** -0.5)
            z_ref[:, hh * LANES:(hh + 1) * LANES] = r.astype(BF16)

    @pl.when(j == 1)
    def _():
        z_ref[...] = acc.astype(BF16)

    @pl.when(j == 2)
    def _():
        z_ref[...] = (acc * _sigmoid(acc)).astype(BF16)

    def qk_norm_rope(gain, scale, emit):
        c = cd_ref[...]
        s = sd_ref[...]
        lane = lax.broadcasted_iota(jnp.int32, c.shape, 1)
        first_half = (lane % D_HEAD) < (D_HEAD // 2)
        for t in range(SECTION // NORM_GROUP_TILE):
            blk = acc[:, t * NORM_GROUP_TILE:(t + 1) * NORM_GROUP_TILE]
            ms = jnp.dot((blk * blk).astype(BF16), bd_ref[...], preferred_element_type=F32) * (1.0 / D_HEAD)
            y = blk * lax.rsqrt(ms + EPS)
            for u in range(NORM_GROUP_TILE // LANES):
                yb = y[:, u * LANES:(u + 1) * LANES] * gain
                partner = jnp.where(first_half,
                                    pltpu.roll(yb, LANES - D_HEAD // 2, 1),
                                    pltpu.roll(yb, D_HEAD // 2, 1))
                r = yb * c + partner * s
                if scale != 1.0:
                    r = r * scale
                col = t * NORM_GROUP_TILE + u * LANES
                emit(col, r)

    @pl.when(j == 3)
    def _():
        def emit(col, r):
            z_ref[:, col:col + LANES] = r.astype(BF16)
        qk_norm_rope(qg_ref[...], D_HEAD ** -0.5, emit)

    @pl.when(j == 4)
    def _():
        def emit(col, r):
            k32_ref[:, col // LANES, :] = r
            z_ref[:, col:col + LANES] = r.astype(BF16)
        qk_norm_rope(kg_ref[...], 1.0, emit)

    @pl.when(j == 5)
    def _():
        for hh in range(H_DIFF):
            v32_ref[:, hh, :] = acc[:, hh * DV_DIFF:(hh + 1) * DV_DIFF]
        z_ref[...] = acc.astype(BF16)

    @pl.when(j >= 6)
    def _():
        z_ref[...] = _sigmoid(acc).astype(BF16)


def _in_projection(x, g, w_bf16, tabs, qg, kg, bd, table_rows):
    n = x.shape[0]
    tm = min(TOKEN_TILE, n)
    n_tab_blocks = table_rows // tm
    tab_spec = pl.BlockSpec((tm, LANES), lambda i, j: (i % n_tab_blocks, 0))
    const = lambda shape: pl.BlockSpec(shape, lambda i, j: (0, 0))
    return pl.pallas_call(
        _inproj_kernel,
        grid=(n // tm, N_SECTIONS),
        in_specs=[
            pl.BlockSpec((tm, D_MODEL), lambda i, j: (i, 0)),
            const((1, D_MODEL)),
            pl.BlockSpec((D_MODEL, SECTION), lambda i, j: (0, j)),
            tab_spec, tab_spec, tab_spec, tab_spec,
            const((1, LANES)), const((1, LANES)),
            const((NORM_GROUP_TILE, NORM_GROUP_TILE)),
        ],
        out_specs=[
            pl.BlockSpec((tm, SECTION), lambda i, j: (i, j)),
            pl.BlockSpec((tm, H_DIFF, DV_DIFF), lambda i, j: (i, 0, 0)),
            pl.BlockSpec((tm, H_DIFF, DV_DIFF), lambda i, j: (i, 0, 0)),
        ],
        out_shape=[
            jax.ShapeDtypeStruct((n, N_SECTIONS * SECTION), BF16),
            jax.ShapeDtypeStruct((n, H_DIFF, DV_DIFF), F32),
            jax.ShapeDtypeStruct((n, H_DIFF, DV_DIFF), F32),
        ],
        scratch_shapes=[pltpu.VMEM((tm, D_MODEL), BF16)],
        compiler_params=_params(("parallel", "arbitrary")),
        name="in_projection",
    )(x, g, w_bf16, *tabs, qg, kg, bd)


def _retention_kernel(lg_ref, q_ref, k_ref, v_ref, sg_ref, gn_ref, s0_ref, y_ref, s_ref, decay_ref):
    c = pl.program_id(2)
    chunk = q_ref.shape[0]
    lg = lg_ref[...][:, :1]

    @pl.when(c == 0)
    def _():
        s_ref[...] = s0_ref[...]
        row = lax.broadcasted_iota(jnp.int32, (chunk, chunk), 0)
        col = lax.broadcasted_iota(jnp.int32, (chunk, chunk), 1)
        dist = (row - col).astype(F32)
        decay_ref[...] = jnp.where(dist >= 0, jnp.exp(jnp.maximum(dist, 0.0) * lg), 0.0)

    idx = lax.broadcasted_iota(jnp.int32, (chunk, 1), 0).astype(F32)
    xi = jnp.exp((idx + 1.0) * lg)
    zeta = jnp.exp((chunk - 1.0 - idx) * lg)
    chunk_decay = jnp.exp(chunk * lg)

    q = q_ref[...]
    k = k_ref[...]
    v = v_ref[...]
    state = s_ref[...]
    att = lax.dot_general(q, k, NT_DIMS, preferred_element_type=F32) * decay_ref[...]
    o = (jnp.dot(att.astype(BF16), v, preferred_element_type=F32)
         + jnp.dot(q, state.astype(BF16), preferred_element_type=F32) * xi)
    kz = (k.astype(F32) * zeta).astype(BF16)
    s_ref[...] = chunk_decay * state + lax.dot_general(kz, v, TN_DIMS, preferred_element_type=F32)

    ms = jnp.mean(o * o, axis=-1, keepdims=True)
    y = o * lax.rsqrt(ms + EPS) * gn_ref[...] * sg_ref[...].astype(F32)
    y_ref[...] = y.astype(BF16)


def _retention_prompt(z16, lg_tab, gn, s0, batch, seq):
    chunk = min(RET_CHUNK, seq)
    n_chunks = seq // chunk
    row = lambda b, h, c: b * n_chunks + c
    return pl.pallas_call(
        _retention_kernel,
        grid=(batch, H_RET, n_chunks),
        in_specs=[
            pl.BlockSpec((None, 1, LANES), lambda b, h, c: (h, 0, 0)),
            pl.BlockSpec((chunk, DK_RET), lambda b, h, c: (row(b, h, c), h)),
            pl.BlockSpec((chunk, DK_RET), lambda b, h, c: (row(b, h, c), H_RET + h)),
            pl.BlockSpec((chunk, DV_RET), lambda b, h, c: (row(b, h, c), SECTION // DV_RET + h)),
            pl.BlockSpec((chunk, DV_RET), lambda b, h, c: (row(b, h, c), 2 * SECTION // DV_RET + h)),
            pl.BlockSpec((None, 1, DV_RET), lambda b, h, c: (h, 0, 0)),
            pl.BlockSpec((None, None, DK_RET, DV_RET), lambda b, h, c: (b, h, 0, 0)),
        ],
        out_specs=[
            pl.BlockSpec((chunk, DV_RET), lambda b, h, c: (row(b, h, c), h)),
            pl.BlockSpec((None, None, DK_RET, DV_RET), lambda b, h, c: (b, h, 0, 0)),
        ],
        out_shape=[
            jax.ShapeDtypeStruct((batch * seq, H_RET * DV_RET), BF16),
            jax.ShapeDtypeStruct((batch, H_RET, DK_RET, DV_RET), F32),
        ],
        scratch_shapes=[pltpu.VMEM((chunk, chunk), F32)],
        compiler_params=_params(("parallel", "parallel", "arbitrary")),
        name="retention_prompt",
    )(lg_tab, z16, z16, z16, z16, gn, s0)


def _retention_sample_kernel(lg_ref, q_ref, k_ref, v_ref, sg_ref, gn_ref, s0_ref, y_ref, s_ref, *, ts):
    rows = q_ref.shape[0]
    group = rows // ts
    row = lax.broadcasted_iota(jnp.int32, (rows, rows), 0)
    col = lax.broadcasted_iota(jnp.int32, (rows, rows), 1)
    same_seq = (row // ts) == (col // ts)
    dist = ((row % ts) - (col % ts)).astype(F32)
    ridx = lax.broadcasted_iota(jnp.int32, (rows, 1), 0)
    tpos = (ridx % ts).astype(F32)
    seq_of_row = ridx // ts

    for h in range(H_RET):
        lg = lg_ref[h][:, :1]
        decay = jnp.where(same_seq & (dist >= 0), jnp.exp(jnp.maximum(dist, 0.0) * lg), 0.0)
        xi = jnp.exp((tpos + 1.0) * lg)
        zeta = jnp.exp((ts - 1.0 - tpos) * lg)
        seq_decay = jnp.exp(ts * lg)
        q = q_ref[:, h * DK_RET:(h + 1) * DK_RET]
        k = k_ref[:, h * DK_RET:(h + 1) * DK_RET]
        v = v_ref[:, h * DV_RET:(h + 1) * DV_RET]
        att = lax.dot_general(q, k, NT_DIMS, preferred_element_type=F32) * decay
        o = jnp.dot(att.astype(BF16), v, preferred_element_type=F32)
        kz = k.astype(F32) * zeta
        for bb in range(group):
            mine = seq_of_row == bb
            state = s0_ref[bb, h]
            full = jnp.dot(q, state.astype(BF16), preferred_element_type=F32) * xi
            o = o + jnp.where(mine, full, 0.0)
            kz_b = jnp.where(mine, kz, 0.0).astype(BF16)
            s_ref[bb, h] = seq_decay * state + lax.dot_general(kz_b, v, TN_DIMS, preferred_element_type=F32)
        ms = jnp.mean(o * o, axis=-1, keepdims=True)
        y = o * lax.rsqrt(ms + EPS) * gn_ref[h] * sg_ref[:, h * DV_RET:(h + 1) * DV_RET].astype(F32)
        y_ref[:, h * DV_RET:(h + 1) * DV_RET] = y.astype(BF16)


def _retention_sample(z16, lg_tab, gn, s0, batch, ts):
    group = min(RET_SAMPLE_GROUP, batch)
    rows = group * ts
    return pl.pallas_call(
        functools.partial(_retention_sample_kernel, ts=ts),
        grid=(batch // group,),
        in_specs=[
            pl.BlockSpec((H_RET, 1, LANES), lambda i: (0, 0, 0)),
            pl.BlockSpec((rows, H_RET * DK_RET), lambda i: (i, 0)),
            pl.BlockSpec((rows, H_RET * DK_RET), lambda i: (i, 1)),
            pl.BlockSpec((rows, H_RET * DV_RET), lambda i: (i, 1)),
            pl.BlockSpec((rows, H_RET * DV_RET), lambda i: (i, 2)),
            pl.BlockSpec((H_RET, 1, DV_RET), lambda i: (0, 0, 0)),
            pl.BlockSpec((group, H_RET, DK_RET, DV_RET), lambda i: (i, 0, 0, 0)),
        ],
        out_specs=[
            pl.BlockSpec((rows, H_RET * DV_RET), lambda i: (i, 0)),
            pl.BlockSpec((group, H_RET, DK_RET, DV_RET), lambda i: (i, 0, 0, 0)),
        ],
        out_shape=[
            jax.ShapeDtypeStruct((batch * ts, H_RET * DV_RET), BF16),
            jax.ShapeDtypeStruct((batch, H_RET, DK_RET, DV_RET), F32),
        ],
        compiler_params=_params(("parallel",)),
        name="retention_sample",
    )(lg_tab, z16, z16, z16, z16, gn, s0)


def _lambda_value(lam_ref, lam_init):
    lv = lam_ref[...]
    a = jnp.sum(lv[0:1] * lv[1:2], axis=-1, keepdims=True)
    b = jnp.sum(lv[2:3] * lv[3:4], axis=-1, keepdims=True)
    return jnp.exp(a) - jnp.exp(b) + lam_init


def _head_norm(o, gain, lam_init):
    ms = jnp.mean(o * o, axis=-1, keepdims=True)
    return o * lax.rsqrt(ms + EPS) * gain * (1.0 - lam_init)


def _diff_prompt_kernel(lam_ref, q_ref, k_ref, v_ref, gn_ref, y_ref, m_ref, l_ref, acc_ref, *, lam_init):
    qi = pl.program_id(2)
    tile = q_ref.shape[0]
    q = q_ref[...]
    lane = lax.broadcasted_iota(jnp.int32, q.shape, 1)
    zero = jnp.zeros_like(q)
    qs = jnp.concatenate([jnp.where(lane < D_HEAD, q, zero), jnp.where(lane >= D_HEAD, q, zero)], axis=0)

    m_ref[...] = jnp.full(m_ref.shape, -jnp.inf, F32)
    l_ref[...] = jnp.zeros(l_ref.shape, F32)
    acc_ref[...] = jnp.zeros(acc_ref.shape, F32)

    def block(kb, masked):
        off = pl.multiple_of(kb * tile, tile)
        k = k_ref[pl.ds(off, tile), :]
        v = v_ref[pl.ds(off, tile), :]
        s = lax.dot_general(qs, k, NT_DIMS, preferred_element_type=F32)
        if masked:
            r = lax.broadcasted_iota(jnp.int32, s.shape, 0) % tile
            cidx = lax.broadcasted_iota(jnp.int32, s.shape, 1)
            s = jnp.where(cidx <= r, s, -jnp.inf)
        m_prev = m_ref[...]
        m_new = jnp.maximum(m_prev, jnp.max(s, axis=-1, keepdims=True))
        alpha = jnp.exp(m_prev - m_new)
        p = jnp.exp(s - m_new)
        l_ref[...] = alpha * l_ref[...] + jnp.sum(p, axis=-1, keepdims=True)
        acc_ref[...] = alpha * acc_ref[...] + jnp.dot(p.astype(BF16), v, preferred_element_type=F32)
        m_ref[...] = m_new

    def body(kb, carry):
        block(kb, False)
        return carry

    lax.fori_loop(0, qi, body, 0)
    block(qi, True)

    on = acc_ref[...] / l_ref[...]
    lam = _lambda_value(lam_ref, lam_init)
    o = on[:tile] - lam * on[tile:]
    y_ref[...] = _head_norm(o, gn_ref[...], lam_init).astype(BF16)


def _diff_prompt(z16, lam_vecs, gn, batch, seq, lam_init):
    tile = min(ATTN_TILE, seq)
    nq = seq // tile
    q_col = 3 * SECTION // LANES
    k_col = 4 * SECTION // LANES
    v_col = 5 * SECTION // LANES
    return pl.pallas_call(
        functools.partial(_diff_prompt_kernel, lam_init=lam_init),
        grid=(batch, H_DIFF, nq),
        in_specs=[
            pl.BlockSpec((4, D_HEAD), lambda b, h, i: (0, 0)),
            pl.BlockSpec((tile, LANES), lambda b, h, i: (b * nq + i, q_col + h)),
            pl.BlockSpec((seq, LANES), lambda b, h, i: (b, k_col + h)),
            pl.BlockSpec((seq, LANES), lambda b, h, i: (b, v_col + h)),
            pl.BlockSpec((1, DV_DIFF), lambda b, h, i: (0, 0)),
        ],
        out_specs=pl.BlockSpec((tile, DV_DIFF), lambda b, h, i: (b * nq + i, h)),
        out_shape=jax.ShapeDtypeStruct((batch * seq, H_DIFF * DV_DIFF), BF16),
        scratch_shapes=[
            pltpu.VMEM((2 * tile, 1), F32),
            pltpu.VMEM((2 * tile, 1), F32),
            pltpu.VMEM((2 * tile, DV_DIFF), F32),
        ],
        compiler_params=_params(("parallel", "parallel", "arbitrary")),
        name="diff_attention_prompt",
    )(lam_vecs, z16, z16, z16, gn)


def _diff_sample_kernel(pt_ref, lam_ref, q_ref, kn_ref, vn_ref, gn_ref, *rest, ts, lam_init):
    del pt_ref
    pages = PAGES_PER_STEP
    k_refs = rest[:pages]
    v_refs = rest[pages:2 * pages]
    y_ref, q_scr, m_ref, l_ref, acc_ref = rest[2 * pages:]
    p_step = pl.program_id(1)
    hr = 2 * ts
    rows = H_DIFF * hr

    def head_cols(x, h):
        return x[:, h * DV_DIFF:(h + 1) * DV_DIFF]

    @pl.when(p_step == 0)
    def _():
        q32 = q_ref[...].astype(F32)
        r8 = lax.broadcasted_iota(jnp.int32, (hr, LANES), 0)
        lane8 = lax.broadcasted_iota(jnp.int32, (hr, LANES), 1)
        keep = (lane8 // D_HEAD) == (r8 // ts)
        q_heads = []
        for h in range(H_DIFF):
            blk = head_cols(q32, h)
            qt = jnp.zeros((hr, LANES), F32)
            for t in range(ts):
                qt = jnp.where(r8 % ts == t, blk[t:t + 1, :], qt)
            q_heads.append(jnp.where(keep, qt, 0.0))
        q_all = jnp.concatenate(q_heads, axis=0)
        q_scr[...] = q_all

        kn = kn_ref[...].astype(F32)
        vn = vn_ref[...].astype(F32)

        def per_row(x, t):
            return jnp.concatenate(
                [jnp.broadcast_to(head_cols(x, h)[t:t + 1, :], (hr, LANES)) for h in range(H_DIFF)], axis=0)

        qpos = lax.broadcasted_iota(jnp.int32, (rows, 1), 0) % ts
        s_new = [jnp.sum(q_all * per_row(kn, t), axis=-1, keepdims=True) for t in range(ts)]
        m0 = s_new[0]
        for t in range(1, ts):
            m0 = jnp.where(qpos >= t, jnp.maximum(m0, s_new[t]), m0)
        l0 = jnp.zeros((rows, 1), F32)
        a0 = jnp.zeros((rows, LANES), F32)
        for t in range(ts):
            pt = jnp.where(qpos >= t, jnp.exp(s_new[t] - m0), 0.0)
            l0 = l0 + pt
            a0 = a0 + pt * per_row(vn, t)
        m_ref[...] = m0
        l_ref[...] = l0
        acc_ref[...] = a0

    def gather_head(refs, h):
        return jnp.concatenate([refs[r][:, h, :] for r in range(pages)], axis=0).astype(BF16)

    q_all = q_scr[...]
    s = jnp.concatenate(
        [lax.dot_general(q_all[h * hr:(h + 1) * hr].astype(BF16), gather_head(k_refs, h), NT_DIMS,
                         preferred_element_type=F32) for h in range(H_DIFF)], axis=0)
    m_prev = m_ref[...]
    m_new = jnp.maximum(m_prev, jnp.max(s, axis=-1, keepdims=True))
    alpha = jnp.exp(m_prev - m_new)
    p = jnp.exp(s - m_new)
    l_ref[...] = alpha * l_ref[...] + jnp.sum(p, axis=-1, keepdims=True)
    pv = jnp.concatenate(
        [jnp.dot(p[h * hr:(h + 1) * hr].astype(BF16), gather_head(v_refs, h), preferred_element_type=F32)
         for h in range(H_DIFF)], axis=0)
    acc_ref[...] = alpha * acc_ref[...] + pv
    m_ref[...] = m_new

    @pl.when(p_step == pl.num_programs(1) - 1)
    def _():
        on = acc_ref[...] / l_ref[...]
        lam = _lambda_value(lam_ref, lam_init)
        gain = gn_ref[...]
        for h in range(H_DIFF):
            o = on[h * hr:h * hr + ts] - lam * on[h * hr + ts:(h + 1) * hr]
            y_ref[:, h * DV_DIFF:(h + 1) * DV_DIFF] = _head_norm(o, gain, lam_init).astype(BF16)


def _diff_sample(z16, cache_k, cache_v, page_table, lam_vecs, gn, batch, ts, lam_init):
    n_pages = page_table.shape[1]
    page = cache_k.shape[1]
    width = H_DIFF * DV_DIFF
    pages = PAGES_PER_STEP
    assert n_pages % pages == 0
    rows = 2 * H_DIFF * ts
    z3 = z16.reshape(batch, ts, N_SECTIONS * SECTION)
    new_spec = lambda sec: pl.BlockSpec((None, ts, SECTION), lambda b, p, pt: (b, 0, sec))

    def page_spec(r):
        return pl.BlockSpec((None, page, H_DIFF, DV_DIFF), lambda b, p, pt: (pt[b, p * pages + r], 0, 0, 0))

    grid_spec = pltpu.PrefetchScalarGridSpec(
        num_scalar_prefetch=1,
        grid=(batch, n_pages // pages),
        in_specs=[
            pl.BlockSpec((4, D_HEAD), lambda b, p, pt: (0, 0)),
            new_spec(3), new_spec(4), new_spec(5),
            pl.BlockSpec((1, DV_DIFF), lambda b, p, pt: (0, 0)),
        ] + [page_spec(r) for r in range(pages)] + [page_spec(r) for r in range(pages)],
        out_specs=pl.BlockSpec((None, ts, width), lambda b, p, pt: (b, 0, 0)),
        scratch_shapes=[
            pltpu.VMEM((rows, LANES), F32),
            pltpu.VMEM((rows, 1), F32),
            pltpu.VMEM((rows, 1), F32),
            pltpu.VMEM((rows, DV_DIFF), F32),
        ],
    )
    y = pl.pallas_call(
        functools.partial(_diff_sample_kernel, ts=ts, lam_init=lam_init),
        grid_spec=grid_spec,
        out_shape=jax.ShapeDtypeStruct((batch, ts, width), BF16),
        compiler_params=_params(("parallel", "arbitrary")),
        name="diff_attention_sample",
    )(page_table, lam_vecs, z3, z3, z3, gn, *([cache_k] * pages), *([cache_v] * pages))
    return y.reshape(batch * ts, width)


def _merge_kernel(x_ref, yr_ref, yd_ref, gr_ref, gd_ref, wr_ref, wd_ref, wo_ref, g_ref, x1_ref, h_ref):
    a = jnp.dot(yr_ref[...], wr_ref[...], preferred_element_type=F32)
    b = jnp.dot(yd_ref[...], wd_ref[...], preferred_element_type=F32)
    m = gr_ref[...].astype(F32) * a + gd_ref[...].astype(F32) * b
    x1 = x_ref[...] + jnp.dot(m.astype(BF16), wo_ref[...], preferred_element_type=F32)
    x1_ref[...] = x1
    ms = jnp.mean(x1 * x1, axis=-1, keepdims=True)
    h_ref[...] = (x1 * lax.rsqrt(ms + EPS) * g_ref[...]).astype(BF16)


def _merge(x, y_r, y_d, z16, w_br_ret, w_br_diff, w_out, g_ffn):
    n = x.shape[0]
    tm = min(TOKEN_TILE, n)
    tile = lambda col: pl.BlockSpec((tm, D_MODEL), lambda i: (i, col))
    weight = pl.BlockSpec((D_MODEL, D_MODEL), lambda i: (0, 0))
    return pl.pallas_call(
        _merge_kernel,
        grid=(n // tm,),
        in_specs=[tile(0), tile(0), tile(0), tile(6), tile(7), weight, weight, weight,
                  pl.BlockSpec((1, D_MODEL), lambda i: (0, 0))],
        out_specs=[tile(0), tile(0)],
        out_shape=[jax.ShapeDtypeStruct((n, D_MODEL), F32), jax.ShapeDtypeStruct((n, D_MODEL), BF16)],
        compiler_params=_params(("parallel",)),
        name="merge",
    )(x, y_r, y_d, z16, z16, w_br_ret, w_br_diff, w_out, g_ffn)


def _ffn_chunks(d_ff):
    edges = list(range(0, d_ff, SECTION)) + [d_ff]
    return list(zip(edges[:-1], edges[1:]))


def _ffn_kernel(x_ref, h_ref, wg_ref, wu_ref, wd_ref, o_ref):
    h = h_ref[...]
    acc = x_ref[...]
    for c0, c1 in _ffn_chunks(wg_ref.shape[1]):
        g = jnp.dot(h, wg_ref[:, c0:c1], preferred_element_type=F32)
        u = jnp.dot(h, wu_ref[:, c0:c1], preferred_element_type=F32)
        a = (g * _sigmoid(g) * u).astype(BF16)
        acc = acc + jnp.dot(a, wd_ref[c0:c1, :], preferred_element_type=F32)
    o_ref[...] = acc


def _ffn(x1, h, wg, wu, wd):
    n = x1.shape[0]
    d_ff = wg.shape[1]
    tm = min(TOKEN_TILE, n)
    tile = pl.BlockSpec((tm, D_MODEL), lambda i: (i, 0))
    single = pl.Buffered(1)
    return pl.pallas_call(
        _ffn_kernel,
        grid=(n // tm,),
        in_specs=[tile, tile,
                  pl.BlockSpec((D_MODEL, d_ff), lambda i: (0, 0), pipeline_mode=single),
                  pl.BlockSpec((D_MODEL, d_ff), lambda i: (0, 0), pipeline_mode=single),
                  pl.BlockSpec((d_ff, D_MODEL), lambda i: (0, 0), pipeline_mode=single)],
        out_specs=tile,
        out_shape=jax.ShapeDtypeStruct((n, D_MODEL), F32),
        compiler_params=_params(("parallel",)),
        name="ffn",
    )(x1, h, wg, wu, wd)


def _rope_tables(pos, half):
    inv = 1.0 / jnp.power(ROPE_THETA, jnp.arange(half, dtype=F32) / half)
    ang = pos.astype(F32)[:, None] * inv[None, :]
    cos = jnp.cos(ang)
    sin = jnp.sin(ang)
    reps = LANES // (2 * half)
    return (jnp.tile(jnp.concatenate([cos, cos], axis=-1), (1, reps)),
            jnp.tile(jnp.concatenate([-sin, sin], axis=-1), (1, reps)))


def _layer_tables(pos):
    cr, sr = _rope_tables(pos, DK_RET // 2)
    cd, sd = _rope_tables(pos, D_HEAD // 2)
    return (cr, sr, cd, sd)


def kernel(x_prompt, x_sample, cache_k, cache_v, state_ret, page_table, norm_mix_g, w_in, ret_gn_g, diff_qnorm_g, diff_knorm_g, lambda_q1, lambda_k1, lambda_q2, lambda_k2, diff_gn_g, w_br_ret, w_br_diff, w_out, norm_ffn_g, w_ffn_gate, w_ffn_up, w_ffn_down):
    B, T, D = x_prompt.shape
    Bs, Ts, _ = x_sample.shape
    depth, n_phys, page = cache_k.shape[:3]
    past_len = page_table.shape[1] * page

    tabs_p = _layer_tables(jnp.arange(T, dtype=jnp.int32))
    tabs_s = _layer_tables(jnp.tile(past_len + jnp.arange(Ts, dtype=jnp.int32), Bs))
    lg = jnp.log(1.0 - jnp.power(2.0, -5.0 - jnp.arange(H_RET, dtype=F32)))
    lg_tab = jnp.broadcast_to(lg[:, None, None], (H_RET, 1, LANES))
    gidx = jnp.arange(NORM_GROUP_TILE) // D_HEAD
    bd = (gidx[:, None] == gidx[None, :]).astype(BF16)
    ck = cache_k.reshape(depth * n_phys, page, H_DIFF, DV_DIFF)
    cv = cache_v.reshape(depth * n_phys, page, H_DIFF, DV_DIFF)

    xp = x_prompt.reshape(B * T, D)
    xs = x_sample.reshape(Bs * Ts, D)
    outs = [[] for _ in range(6)]
    for l in range(depth):
        lam_init = 0.8 - 0.6 * math.exp(-0.3 * l)
        w_in_l = w_in[l].astype(BF16)
        g_mix = norm_mix_g[l].reshape(1, D)
        g_ffn = norm_ffn_g[l].reshape(1, D)
        qg = jnp.tile(diff_qnorm_g[l], LANES // D_HEAD).reshape(1, LANES)
        kg = jnp.tile(diff_knorm_g[l], LANES // D_HEAD).reshape(1, LANES)
        gn_r = ret_gn_g[l].reshape(H_RET, 1, DV_RET)
        gn_d = diff_gn_g[l].reshape(1, DV_DIFF)
        lam_vecs = jnp.stack([lambda_q1[l], lambda_k1[l], lambda_q2[l], lambda_k2[l]]).astype(F32)
        wr = w_br_ret[l].astype(BF16)
        wdf = w_br_diff[l].astype(BF16)
        wo = w_out[l].astype(BF16)
        wg = w_ffn_gate[l].astype(BF16)
        wu = w_ffn_up[l].astype(BF16)
        wd = w_ffn_down[l].astype(BF16)

        z16, k32, v32 = _in_projection(xp, g_mix, w_in_l, tabs_p, qg, kg, bd, T)
        y_r, s_p = _retention_prompt(z16, lg_tab, gn_r, jnp.zeros((B, H_RET, DK_RET, DV_RET), F32), B, T)
        y_d = _diff_prompt(z16, lam_vecs, gn_d, B, T, lam_init)
        x1, h2 = _merge(xp, y_r, y_d, z16, wr, wdf, wo, g_ffn)
        xp = _ffn(x1, h2, wg, wu, wd)
        outs[0].append(k32.reshape(B, T, H_DIFF, 2 * D_HEAD))
        outs[1].append(v32.reshape(B, T, H_DIFF, DV_DIFF))
        outs[2].append(s_p)

        z16, k32, v32 = _in_projection(xs, g_mix, w_in_l, tabs_s, qg, kg, bd, Bs * Ts)
        y_r, s_s = _retention_sample(z16, lg_tab, gn_r, state_ret[l], Bs, Ts)
        y_d = _diff_sample(z16, ck, cv, page_table + l * n_phys, lam_vecs, gn_d, Bs, Ts, lam_init)
        x1, h2 = _merge(xs, y_r, y_d, z16, wr, wdf, wo, g_ffn)
        xs = _ffn(x1, h2, wg, wu, wd)
        outs[3].append(k32.reshape(Bs, Ts, H_DIFF, 2 * D_HEAD))
        outs[4].append(v32.reshape(Bs, Ts, H_DIFF, DV_DIFF))
        outs[5].append(s_s)

    stacked = [jnp.stack(o) for o in outs]
    return (xp.reshape(B, T, D), xs.reshape(Bs, Ts, D),
            stacked[0], stacked[1], stacked[2], stacked[3], stacked[4], stacked[5])
```

```python
import functools
import math

import jax
import jax.numpy as jnp
from jax import lax
from jax.experimental import pallas as pl
from jax.experimental.pallas import tpu as pltpu

F32 = jnp.float32
BF16 = jnp.bfloat16

D_MODEL = 1024
H_RET = 4
DK_RET = 128
DV_RET = 256
H_DIFF = 8
D_HEAD = 64
DV_DIFF = 2 * D_HEAD
ROPE_THETA = 10000.0
EPS = 1e-6
LOG2E = 1.4426950408889634
SAFE_LOGIT_BOUND = 64.0

LANES = 128
SECTION = 1024
N_SECTIONS = 8
NORM_GROUP_TILE = 256
VMEM_LIMIT = 56 * 1024 * 1024

TOKEN_TILE = 512
RET_CHUNK = 512
ATTN_TILE = 512
ATTN_KV_TILE = 512
PAGES_PER_STEP = 8
RET_SAMPLE_GROUP = 8

NT_DIMS = (((1,), (1,)), ((), ()))
TN_DIMS = (((0,), (0,)), ((), ()))


def _sigmoid(x):
    return 1.0 / (1.0 + jnp.exp(-x))


def _params(semantics):
    return pltpu.CompilerParams(dimension_semantics=semantics, vmem_limit_bytes=VMEM_LIMIT)


def _inproj_kernel(x_ref, g_ref, w_ref, cr_ref, sr_ref, cd_ref, sd_ref, qg_ref, kg_ref, bd_ref,
                   z_ref, k32_ref, v32_ref, h_ref):
    j = pl.program_id(1)

    @pl.when(j == 0)
    def _():
        x = x_ref[...]
        ms = jnp.mean(x * x, axis=-1, keepdims=True)
        h_ref[...] = (x * lax.rsqrt(ms + EPS) * g_ref[...]).astype(BF16)

    acc = jnp.dot(h_ref[...], w_ref[...], preferred_element_type=F32)

    @pl.when(j == 0)
    def _():
        c = cr_ref[...]
        s = sr_ref[...]
        for hh in range(SECTION // LANES):
            blk = acc[:, hh * LANES:(hh + 1) * LANES]
            r = blk * c + pltpu.roll(blk, DK_RET // 2, 1) * s
            if hh >= H_RET:
                r = r * (DK_RET ** -0.5)
            z_ref[:, hh * LANES:(hh + 1) * LANES] = r.astype(BF16)

    @pl.when(j == 1)
    def _():
        z_ref[...] = acc.astype(BF16)

    @pl.when(j == 2)
    def _():
        z_ref[...] = (acc * _sigmoid(acc)).astype(BF16)

    def qk_norm_rope(gain, scale, emit):
        c = cd_ref[...]
        s = sd_ref[...]
        lane = lax.broadcasted_iota(jnp.int32, c.shape, 1)
        first_half = (lane % D_HEAD) < (D_HEAD // 2)
        for t in range(SECTION // NORM_GROUP_TILE):
            blk = acc[:, t * NORM_GROUP_TILE:(t + 1) * NORM_GROUP_TILE]
            ms = jnp.dot((blk * blk).astype(BF16), bd_ref[...], preferred_element_type=F32) * (1.0 / D_HEAD)
            y = blk * lax.rsqrt(ms + EPS)
            for u in range(NORM_GROUP_TILE // LANES):
                yb = y[:, u * LANES:(u + 1) * LANES] * gain
                partner = jnp.where(first_half,
                                    pltpu.roll(yb, LANES - D_HEAD // 2, 1),
                                    pltpu.roll(yb, D_HEAD // 2, 1))
                r = yb * c + partner * s
                if scale != 1.0:
                    r = r * scale
                col = t * NORM_GROUP_TILE + u * LANES
                emit(col, r)

    @pl.when(j == 3)
    def _():
        def emit(col, r):
            z_ref[:, col:col + LANES] = r.astype(BF16)
        qk_norm_rope(qg_ref[...], D_HEAD ** -0.5 * LOG2E, emit)

    @pl.when(j == 4)
    def _():
        def emit(col, r):
            k32_ref[:, col // LANES, :] = r
            z_ref[:, col:col + LANES] = r.astype(BF16)
        qk_norm_rope(kg_ref[...], 1.0, emit)

    @pl.when(j == 5)
    def _():
        for hh in range(H_DIFF):
            v32_ref[:, hh, :] = acc[:, hh * DV_DIFF:(hh + 1) * DV_DIFF]
        z_ref[...] = acc.astype(BF16)

    @pl.when(j >= 6)
    def _():
        z_ref[...] = _sigmoid(acc).astype(BF16)


def _in_projection(x, g, w_bf16, tabs, qg, kg, bd, table_rows):
    n = x.shape[0]
    tm = min(TOKEN_TILE, n)
    n_tab_blocks = table_rows // tm
    tab_spec = pl.BlockSpec((tm, LANES), lambda i, j: (i % n_tab_blocks, 0))
    const = lambda shape: pl.BlockSpec(shape, lambda i, j: (0, 0))
    return pl.pallas_call(
        _inproj_kernel,
        grid=(n // tm, N_SECTIONS),
        in_specs=[
            pl.BlockSpec((tm, D_MODEL), lambda i, j: (i, 0)),
            const((1, D_MODEL)),
            pl.BlockSpec((D_MODEL, SECTION), lambda i, j: (0, j)),
            tab_spec, tab_spec, tab_spec, tab_spec,
            const((1, LANES)), const((1, LANES)),
            const((NORM_GROUP_TILE, NORM_GROUP_TILE)),
        ],
        out_specs=[
            pl.BlockSpec((tm, SECTION), lambda i, j: (i, j)),
            pl.BlockSpec((tm, H_DIFF, DV_DIFF), lambda i, j: (i, 0, 0)),
            pl.BlockSpec((tm, H_DIFF, DV_DIFF), lambda i, j: (i, 0, 0)),
        ],
        out_shape=[
            jax.ShapeDtypeStruct((n, N_SECTIONS * SECTION), BF16),
            jax.ShapeDtypeStruct((n, H_DIFF, DV_DIFF), F32),
            jax.ShapeDtypeStruct((n, H_DIFF, DV_DIFF), F32),
        ],
        scratch_shapes=[pltpu.VMEM((tm, D_MODEL), BF16)],
        compiler_params=_params(("parallel", "arbitrary")),
        name="in_projection",
    )(x, g, w_bf16, *tabs, qg, kg, bd)


def _retention_kernel(lg_ref, q_ref, k_ref, v_ref, sg_ref, gn_ref, s0_ref, y_ref, s_ref, decay_ref):
    c = pl.program_id(2)
    chunk = q_ref.shape[0]
    lg = lg_ref[...][:, :1]

    @pl.when(c == 0)
    def _():
        s_ref[...] = s0_ref[...]
        row = lax.broadcasted_iota(jnp.int32, (chunk, chunk), 0)
        col = lax.broadcasted_iota(jnp.int32, (chunk, chunk), 1)
        dist = (row - col).astype(F32)
        decay_ref[...] = jnp.where(dist >= 0, jnp.exp(jnp.maximum(dist, 0.0) * lg), 0.0)

    idx = lax.broadcasted_iota(jnp.int32, (chunk, 1), 0).astype(F32)
    xi = jnp.exp((idx + 1.0) * lg)
    zeta = jnp.exp((chunk - 1.0 - idx) * lg)
    chunk_decay = jnp.exp(chunk * lg)

    q = q_ref[...]
    k = k_ref[...]
    v = v_ref[...]
    state = s_ref[...]
    att = lax.dot_general(q, k, NT_DIMS, preferred_element_type=F32) * decay_ref[...]
    o = (jnp.dot(att.astype(BF16), v, preferred_element_type=F32)
         + jnp.dot(q, state.astype(BF16), preferred_element_type=F32) * xi)
    kz = (k.astype(F32) * zeta).astype(BF16)
    s_ref[...] = chunk_decay * state + lax.dot_general(kz, v, TN_DIMS, preferred_element_type=F32)

    ms = jnp.mean(o * o, axis=-1, keepdims=True)
    y = o * lax.rsqrt(ms + EPS) * gn_ref[...] * sg_ref[...].astype(F32)
    y_ref[...] = y.astype(BF16)


def _retention_prompt(z16, lg_tab, gn, s0, batch, seq):
    chunk = min(RET_CHUNK, seq)
    n_chunks = seq // chunk
    row = lambda b, h, c: b * n_chunks + c
    return pl.pallas_call(
        _retention_kernel,
        grid=(batch, H_RET, n_chunks),
        in_specs=[
            pl.BlockSpec((None, 1, LANES), lambda b, h, c: (h, 0, 0)),
            pl.BlockSpec((chunk, DK_RET), lambda b, h, c: (row(b, h, c), h)),
            pl.BlockSpec((chunk, DK_RET), lambda b, h, c: (row(b, h, c), H_RET + h)),
            pl.BlockSpec((chunk, DV_RET), lambda b, h, c: (row(b, h, c), SECTION // DV_RET + h)),
            pl.BlockSpec((chunk, DV_RET), lambda b, h, c: (row(b, h, c), 2 * SECTION // DV_RET + h)),
            pl.BlockSpec((None, 1, DV_RET), lambda b, h, c: (h, 0, 0)),
            pl.BlockSpec((None, None, DK_RET, DV_RET), lambda b, h, c: (b, h, 0, 0)),
        ],
        out_specs=[
            pl.BlockSpec((chunk, DV_RET), lambda b, h, c: (row(b, h, c), h)),
            pl.BlockSpec((None, None, DK_RET, DV_RET), lambda b, h, c: (b, h, 0, 0)),
        ],
        out_shape=[
            jax.ShapeDtypeStruct((batch * seq, H_RET * DV_RET), BF16),
            jax.ShapeDtypeStruct((batch, H_RET, DK_RET, DV_RET), F32),
        ],
        scratch_shapes=[pltpu.VMEM((chunk, chunk), F32)],
        compiler_params=_params(("parallel", "parallel", "arbitrary")),
        name="retention_prompt",
    )(lg_tab, z16, z16, z16, z16, gn, s0)


def _retention_sample_kernel(lg_ref, q_ref, k_ref, v_ref, sg_ref, gn_ref, s0_ref, y_ref, s_ref, *, ts):
    rows = q_ref.shape[0]
    group = rows // ts
    row = lax.broadcasted_iota(jnp.int32, (rows, rows), 0)
    col = lax.broadcasted_iota(jnp.int32, (rows, rows), 1)
    same_seq = (row // ts) == (col // ts)
    dist = ((row % ts) - (col % ts)).astype(F32)
    ridx = lax.broadcasted_iota(jnp.int32, (rows, 1), 0)
    tpos = (ridx % ts).astype(F32)
    seq_of_row = ridx // ts

    for h in range(H_RET):
        lg = lg_ref[h][:, :1]
        decay = jnp.where(same_seq & (dist >= 0), jnp.exp(jnp.maximum(dist, 0.0) * lg), 0.0)
        xi = jnp.exp((tpos + 1.0) * lg)
        zeta = jnp.exp((ts - 1.0 - tpos) * lg)
        seq_decay = jnp.exp(ts * lg)
        q = q_ref[:, h * DK_RET:(h + 1) * DK_RET]
        k = k_ref[:, h * DK_RET:(h + 1) * DK_RET]
        v = v_ref[:, h * DV_RET:(h + 1) * DV_RET]
        att = lax.dot_general(q, k, NT_DIMS, preferred_element_type=F32) * decay
        o = jnp.dot(att.astype(BF16), v, preferred_element_type=F32)
        kz = k.astype(F32) * zeta
        for bb in range(group):
            mine = seq_of_row == bb
            state = s0_ref[bb, h]
            full = jnp.dot(q, state.astype(BF16), preferred_element_type=F32) * xi
            o = o + jnp.where(mine, full, 0.0)
            kz_b = jnp.where(mine, kz, 0.0).astype(BF16)
            s_ref[bb, h] = seq_decay * state + lax.dot_general(kz_b, v, TN_DIMS, preferred_element_type=F32)
        ms = jnp.mean(o * o, axis=-1, keepdims=True)
        y = o * lax.rsqrt(ms + EPS) * gn_ref[h] * sg_ref[:, h * DV_RET:(h + 1) * DV_RET].astype(F32)
        y_ref[:, h * DV_RET:(h + 1) * DV_RET] = y.astype(BF16)


def _retention_sample(z16, lg_tab, gn, s0, batch, ts):
    group = min(RET_SAMPLE_GROUP, batch)
    rows = group * ts
    return pl.pallas_call(
        functools.partial(_retention_sample_kernel, ts=ts),
        grid=(batch // group,),
        in_specs=[
            pl.BlockSpec((H_RET, 1, LANES), lambda i: (0, 0, 0)),
            pl.BlockSpec((rows, H_RET * DK_RET), lambda i: (i, 0)),
            pl.BlockSpec((rows, H_RET * DK_RET), lambda i: (i, 1)),
            pl.BlockSpec((rows, H_RET * DV_RET), lambda i: (i, 1)),
            pl.BlockSpec((rows, H_RET * DV_RET), lambda i: (i, 2)),
            pl.BlockSpec((H_RET, 1, DV_RET), lambda i: (0, 0, 0)),
            pl.BlockSpec((group, H_RET, DK_RET, DV_RET), lambda i: (i, 0, 0, 0)),
        ],
        out_specs=[
            pl.BlockSpec((rows, H_RET * DV_RET), lambda i: (i, 0)),
            pl.BlockSpec((group, H_RET, DK_RET, DV_RET), lambda i: (i, 0, 0, 0)),
        ],
        out_shape=[
            jax.ShapeDtypeStruct((batch * ts, H_RET * DV_RET), BF16),
            jax.ShapeDtypeStruct((batch, H_RET, DK_RET, DV_RET), F32),
        ],
        compiler_params=_params(("parallel",)),
        name="retention_sample",
    )(lg_tab, z16, z16, z16, z16, gn, s0)


def _lambda_value(lam_ref, lam_init):
    lv = lam_ref[...]
    a = jnp.sum(lv[0:1] * lv[1:2], axis=-1, keepdims=True)
    b = jnp.sum(lv[2:3] * lv[3:4], axis=-1, keepdims=True)
    return jnp.exp(a) - jnp.exp(b) + lam_init


def _head_norm(o, gain, lam_init):
    ms = jnp.mean(o * o, axis=-1, keepdims=True)
    return o * lax.rsqrt(ms + EPS) * gain * (1.0 - lam_init)


def _diff_prompt_kernel(bound_ref, lam_ref, q_ref, k_ref, v_ref, gn_ref, y_ref, acc_ref, *, lam_init, tk):
    qi = pl.program_id(2)
    tq = q_ref.shape[0]
    rows = 2 * tq
    q = q_ref[...]
    lane = lax.broadcasted_iota(jnp.int32, q.shape, 1)
    zero = jnp.zeros_like(q)
    qs = jnp.concatenate([jnp.where(lane < D_HEAD, q, zero), jnp.where(lane >= D_HEAD, q, zero)], axis=0)
    ones = jnp.ones((tk, LANES), BF16)
    acc_ref[...] = jnp.zeros(acc_ref.shape, F32)

    def lanes(x, n):
        return jnp.concatenate([x] * (n // LANES), axis=1)

    def scores(kb):
        k = k_ref[pl.ds(pl.multiple_of(kb * tk, tk), tk), :]
        return lax.dot_general(qs, k, NT_DIMS, preferred_element_type=F32)

    def values(kb):
        v = v_ref[pl.ds(pl.multiple_of(kb * tk, tk), tk), :]
        return jnp.concatenate([v, ones], axis=1)

    def causal(s):
        qpos = qi * tq + lax.broadcasted_iota(jnp.int32, s.shape, 0) % tq
        kpos = n_full * tk + lax.broadcasted_iota(jnp.int32, s.shape, 1)
        return jnp.where(kpos <= qpos, s, -jnp.inf)

    n_full = (qi * tq) // tk

    unstabilised = bound_ref[0] <= SAFE_LOGIT_BOUND

    def weights(kb, masked):
        s = scores(kb)
        return jnp.exp2(causal(s) if masked else s).astype(BF16)

    def accumulate(kb, p):
        acc_ref[...] += jnp.dot(p, values(kb), preferred_element_type=F32)

    @pl.when(unstabilised & (n_full == 0))
    def _():
        accumulate(0, weights(0, True))

    @pl.when(unstabilised & (n_full > 0))
    def _():
        def body(kb, p):
            p_next = weights(kb + 1, False)
            accumulate(kb, p)
            return p_next

        p = lax.fori_loop(0, n_full - 1, body, weights(0, False))
        p_last = weights(n_full, True)
        accumulate(n_full - 1, p)
        accumulate(n_full, p_last)

    @pl.when(bound_ref[0] > SAFE_LOGIT_BOUND)
    def _():
        def update(kb, s, m_prev):
            m_blk = jnp.max(s, axis=-1, keepdims=True)
            m_new = jnp.maximum(m_prev, jnp.broadcast_to(m_blk, (rows, LANES)))
            alpha = jnp.exp2(m_prev - m_new)
            p = jnp.exp2(s - lanes(m_new, tk)).astype(BF16)
            pv = jnp.dot(p, values(kb), preferred_element_type=F32)
            acc_ref[...] = lanes(alpha, 2 * LANES) * acc_ref[...] + pv
            return m_new

        m = lax.fori_loop(0, n_full, lambda kb, m_prev: update(kb, scores(kb), m_prev),
                          jnp.full((rows, LANES), -jnp.inf, F32))
        update(n_full, causal(scores(n_full)), m)

    acc = acc_ref[...]
    on = acc[:, :DV_DIFF] / acc[:, DV_DIFF:]
    lam = _lambda_value(lam_ref, lam_init)
    o = on[:tq] - lam * on[tq:]
    y_ref[...] = _head_norm(o, gn_ref[...], lam_init).astype(BF16)


def _diff_prompt(z16, logit_bound, lam_vecs, gn, batch, seq, lam_init):
    tq = min(ATTN_TILE, seq)
    tk = min(ATTN_KV_TILE, seq)
    nq = seq // tq
    q_col = 3 * SECTION // LANES
    k_col = 4 * SECTION // LANES
    v_col = 5 * SECTION // LANES
    return pl.pallas_call(
        functools.partial(_diff_prompt_kernel, lam_init=lam_init, tk=tk),
        grid=(batch, H_DIFF, nq),
        in_specs=[
            pl.BlockSpec(memory_space=pltpu.SMEM),
            pl.BlockSpec((4, D_HEAD), lambda b, h, i: (0, 0)),
            pl.BlockSpec((tq, LANES), lambda b, h, i: (b * nq + i, q_col + h)),
            pl.BlockSpec((seq, LANES), lambda b, h, i: (b, k_col + h)),
            pl.BlockSpec((seq, LANES), lambda b, h, i: (b, v_col + h)),
            pl.BlockSpec((1, DV_DIFF), lambda b, h, i: (0, 0)),
        ],
        out_specs=pl.BlockSpec((tq, DV_DIFF), lambda b, h, i: (b * nq + i, h)),
        out_shape=jax.ShapeDtypeStruct((batch * seq, H_DIFF * DV_DIFF), BF16),
        scratch_shapes=[pltpu.VMEM((2 * tq, 2 * DV_DIFF), F32)],
        compiler_params=_params(("parallel", "parallel", "arbitrary")),
        name="diff_attention_prompt",
    )(logit_bound, lam_vecs, z16, z16, z16, gn)


def _diff_sample_kernel(pt_ref, lam_ref, q_ref, kn_ref, vn_ref, gn_ref, *rest, ts, lam_init):
    del pt_ref
    pages = PAGES_PER_STEP
    k_refs = rest[:pages]
    v_refs = rest[pages:2 * pages]
    y_ref, q_scr, bias_scr, m_ref, l_ref, acc_ref = rest[2 * pages:]
    p_step = pl.program_id(1)
    hr = 2 * ts
    rows = H_DIFF * hr

    def head_cols(x, h):
        return x[:, h * DV_DIFF:(h + 1) * DV_DIFF]

    @pl.when(p_step == 0)
    def _():
        q32 = q_ref[...].astype(F32)
        r8 = lax.broadcasted_iota(jnp.int32, (hr, LANES), 0)
        lane8 = lax.broadcasted_iota(jnp.int32, (hr, LANES), 1)
        keep = (lane8 // D_HEAD) == (r8 // ts)
        q_heads = []
        for h in range(H_DIFF):
            blk = head_cols(q32, h)
            qt = jnp.zeros((hr, LANES), F32)
            for t in range(ts):
                qt = jnp.where(r8 % ts == t, blk[t:t + 1, :], qt)
            q_heads.append(jnp.where(keep, qt, 0.0))
        q_all = jnp.concatenate(q_heads, axis=0)
        q_scr[...] = q_all
        brow = lax.broadcasted_iota(jnp.int32, bias_scr.shape, 0)
        bcol = lax.broadcasted_iota(jnp.int32, bias_scr.shape, 1)
        bias_scr[...] = jnp.where(bcol % H_DIFF == brow // hr, 0.0, -jnp.inf)

        kn = kn_ref[...].astype(F32)
        vn = vn_ref[...].astype(F32)

        def per_row(x, t):
            return jnp.concatenate(
                [jnp.broadcast_to(head_cols(x, h)[t:t + 1, :], (hr, LANES)) for h in range(H_DIFF)], axis=0)

        qpos = lax.broadcasted_iota(jnp.int32, (rows, 1), 0) % ts
        s_new = [jnp.sum(q_all * per_row(kn, t), axis=-1, keepdims=True) for t in range(ts)]
        m0 = s_new[0]
        for t in range(1, ts):
            m0 = jnp.where(qpos >= t, jnp.maximum(m0, s_new[t]), m0)
        l0 = jnp.zeros((rows, 1), F32)
        a0 = jnp.zeros((rows, LANES), F32)
        for t in range(ts):
            pt = jnp.where(qpos >= t, jnp.exp2(s_new[t] - m0), 0.0)
            l0 = l0 + pt
            a0 = a0 + pt * per_row(vn, t)
        m_ref[...] = m0
        l_ref[...] = l0
        acc_ref[...] = a0

    qb = q_scr[...].astype(BF16)
    bias = bias_scr[...]
    s = jnp.concatenate(
        [lax.dot_general(qb, k_refs[r][...].astype(BF16), NT_DIMS, preferred_element_type=F32) + bias
         for r in range(pages)], axis=1)
    m_prev = m_ref[...]
    m_new = jnp.maximum(m_prev, jnp.max(s, axis=-1, keepdims=True))
    alpha = jnp.exp2(m_prev - m_new)
    p = jnp.exp2(s - m_new)
    l_ref[...] = alpha * l_ref[...] + jnp.sum(p, axis=-1, keepdims=True)
    pb = p.astype(BF16)
    cols = bias.shape[1]
    pv = jnp.dot(pb[:, :cols], v_refs[0][...].astype(BF16), preferred_element_type=F32)
    for r in range(1, pages):
        pv = pv + jnp.dot(pb[:, r * cols:(r + 1) * cols], v_refs[r][...].astype(BF16),
                          preferred_element_type=F32)
    acc_ref[...] = alpha * acc_ref[...] + pv
    m_ref[...] = m_new

    @pl.when(p_step == pl.num_programs(1) - 1)
    def _():
        on = acc_ref[...] / l_ref[...]
        lam = _lambda_value(lam_ref, lam_init)
        gain = gn_ref[...]
        for h in range(H_DIFF):
            o = on[h * hr:h * hr + ts] - lam * on[h * hr + ts:(h + 1) * hr]
            y_ref[:, h * DV_DIFF:(h + 1) * DV_DIFF] = _head_norm(o, gain, lam_init).astype(BF16)


def _diff_sample(z16, cache_k, cache_v, page_table, lam_vecs, gn, batch, ts, lam_init):
    n_pages = page_table.shape[1]
    page_rows = cache_k.shape[1]
    width = H_DIFF * DV_DIFF
    pages = PAGES_PER_STEP
    assert n_pages % pages == 0
    rows = 2 * H_DIFF * ts
    z3 = z16.reshape(batch, ts, N_SECTIONS * SECTION)
    new_spec = lambda sec: pl.BlockSpec((None, ts, SECTION), lambda b, p, pt: (b, 0, sec))

    def page_spec(r):
        return pl.BlockSpec((None, page_rows, DV_DIFF), lambda b, p, pt: (pt[b, p * pages + r], 0, 0))

    grid_spec = pltpu.PrefetchScalarGridSpec(
        num_scalar_prefetch=1,
        grid=(batch, n_pages // pages),
        in_specs=[
            pl.BlockSpec((4, D_HEAD), lambda b, p, pt: (0, 0)),
            new_spec(3), new_spec(4), new_spec(5),
            pl.BlockSpec((1, DV_DIFF), lambda b, p, pt: (0, 0)),
        ] + [page_spec(r) for r in range(pages)] + [page_spec(r) for r in range(pages)],
        out_specs=pl.BlockSpec((None, ts, width), lambda b, p, pt: (b, 0, 0)),
        scratch_shapes=[
            pltpu.VMEM((rows, LANES), F32),
            pltpu.VMEM((rows, page_rows), F32),
            pltpu.VMEM((rows, 1), F32),
            pltpu.VMEM((rows, 1), F32),
            pltpu.VMEM((rows, DV_DIFF), F32),
        ],
    )
    y = pl.pallas_call(
        functools.partial(_diff_sample_kernel, ts=ts, lam_init=lam_init),
        grid_spec=grid_spec,
        out_shape=jax.ShapeDtypeStruct((batch, ts, width), BF16),
        compiler_params=_params(("parallel", "arbitrary")),
        name="diff_attention_sample",
    )(page_table, lam_vecs, z3, z3, z3, gn, *([cache_k] * pages), *([cache_v] * pages))
    return y.reshape(batch * ts, width)


def _merge_kernel(x_ref, yr_ref, yd_ref, gr_ref, gd_ref, wr_ref, wd_ref, wo_ref, g_ref, x1_ref, h_ref):
    a = jnp.dot(yr_ref[...], wr_ref[...], preferred_element_type=F32)
    b = jnp.dot(yd_ref[...], wd_ref[...], preferred_element_type=F32)
    m = gr_ref[...].astype(F32) * a + gd_ref[...].astype(F32) * b
    x1 = x_ref[...] + jnp.dot(m.astype(BF16), wo_ref[...], preferred_element_type=F32)
    x1_ref[...] = x1
    ms = jnp.mean(x1 * x1, axis=-1, keepdims=True)
    h_ref[...] = (x1 * lax.rsqrt(ms + EPS) * g_ref[...]).astype(BF16)


def _merge(x, y_r, y_d, z16, w_br_ret, w_br_diff, w_out, g_ffn):
    n = x.shape[0]
    tm = min(TOKEN_TILE, n)
    tile = lambda col: pl.BlockSpec((tm, D_MODEL), lambda i: (i, col))
    weight = pl.BlockSpec((D_MODEL, D_MODEL), lambda i: (0, 0))
    return pl.pallas_call(
        _merge_kernel,
        grid=(n // tm,),
        in_specs=[tile(0), tile(0), tile(0), tile(6), tile(7), weight, weight, weight,
                  pl.BlockSpec((1, D_MODEL), lambda i: (0, 0))],
        out_specs=[tile(0), tile(0)],
        out_shape=[jax.ShapeDtypeStruct((n, D_MODEL), F32), jax.ShapeDtypeStruct((n, D_MODEL), BF16)],
        compiler_params=_params(("parallel",)),
        name="merge",
    )(x, y_r, y_d, z16, z16, w_br_ret, w_br_diff, w_out, g_ffn)


def _ffn_chunks(d_ff):
    edges = list(range(0, d_ff, SECTION)) + [d_ff]
    return list(zip(edges[:-1], edges[1:]))


def _ffn_kernel(x_ref, h_ref, wg_ref, wu_ref, wd_ref, o_ref):
    h = h_ref[...]
    acc = x_ref[...]
    for c0, c1 in _ffn_chunks(wg_ref.shape[1]):
        g = jnp.dot(h, wg_ref[:, c0:c1], preferred_element_type=F32)
        u = jnp.dot(h, wu_ref[:, c0:c1], preferred_element_type=F32)
        a = (g * _sigmoid(g) * u).astype(BF16)
        acc = acc + jnp.dot(a, wd_ref[c0:c1, :], preferred_element_type=F32)
    o_ref[...] = acc


def _ffn(x1, h, wg, wu, wd):
    n = x1.shape[0]
    d_ff = wg.shape[1]
    tm = min(TOKEN_TILE, n)
    tile = pl.BlockSpec((tm, D_MODEL), lambda i: (i, 0))
    single = pl.Buffered(1)
    return pl.pallas_call(
        _ffn_kernel,
        grid=(n // tm,),
        in_specs=[tile, tile,
                  pl.BlockSpec((D_MODEL, d_ff), lambda i: (0, 0), pipeline_mode=single),
                  pl.BlockSpec((D_MODEL, d_ff), lambda i: (0, 0), pipeline_mode=single),
                  pl.BlockSpec((d_ff, D_MODEL), lambda i: (0, 0), pipeline_mode=single)],
        out_specs=tile,
        out_shape=jax.ShapeDtypeStruct((n, D_MODEL), F32),
        compiler_params=_params(("parallel",)),
        name="ffn",
    )(x1, h, wg, wu, wd)


def _rope_tables(pos, half):
    inv = 1.0 / jnp.power(ROPE_THETA, jnp.arange(half, dtype=F32) / half)
    ang = pos.astype(F32)[:, None] * inv[None, :]
    cos = jnp.cos(ang)
    sin = jnp.sin(ang)
    reps = LANES // (2 * half)
    return (jnp.tile(jnp.concatenate([cos, cos], axis=-1), (1, reps)),
            jnp.tile(jnp.concatenate([-sin, sin], axis=-1), (1, reps)))


def _layer_tables(pos):
    cr, sr = _rope_tables(pos, DK_RET // 2)
    cd, sd = _rope_tables(pos, D_HEAD // 2)
    return (cr, sr, cd, sd)


def kernel(x_prompt, x_sample, cache_k, cache_v, state_ret, page_table, norm_mix_g, w_in, ret_gn_g, diff_qnorm_g, diff_knorm_g, lambda_q1, lambda_k1, lambda_q2, lambda_k2, diff_gn_g, w_br_ret, w_br_diff, w_out, norm_ffn_g, w_ffn_gate, w_ffn_up, w_ffn_down):
    B, T, D = x_prompt.shape
    Bs, Ts, _ = x_sample.shape
    depth, n_phys, page = cache_k.shape[:3]
    past_len = page_table.shape[1] * page

    tabs_p = _layer_tables(jnp.arange(T, dtype=jnp.int32))
    tabs_s = _layer_tables(jnp.tile(past_len + jnp.arange(Ts, dtype=jnp.int32), Bs))
    lg = jnp.log(1.0 - jnp.power(2.0, -5.0 - jnp.arange(H_RET, dtype=F32)))
    lg_tab = jnp.broadcast_to(lg[:, None, None], (H_RET, 1, LANES))
    gidx = jnp.arange(NORM_GROUP_TILE) // D_HEAD
    bd = (gidx[:, None] == gidx[None, :]).astype(BF16)
    ck = cache_k.reshape(depth * n_phys, page * H_DIFF, DV_DIFF)
    cv = cache_v.reshape(depth * n_phys, page * H_DIFF, DV_DIFF)

    xp = x_prompt.reshape(B * T, D)
    xs = x_sample.reshape(Bs * Ts, D)
    outs = [[] for _ in range(6)]
    for l in range(depth):
        lam_init = 0.8 - 0.6 * math.exp(-0.3 * l)
        w_in_l = w_in[l].astype(BF16)
        g_mix = norm_mix_g[l].reshape(1, D)
        g_ffn = norm_ffn_g[l].reshape(1, D)
        qg = jnp.tile(diff_qnorm_g[l], LANES // D_HEAD).reshape(1, LANES)
        kg = jnp.tile(diff_knorm_g[l], LANES // D_HEAD).reshape(1, LANES)
        gn_r = ret_gn_g[l].reshape(H_RET, 1, DV_RET)
        gn_d = diff_gn_g[l].reshape(1, DV_DIFF)
        logit_bound = (1.02 * LOG2E * D_HEAD ** 0.5 * jnp.max(jnp.abs(diff_qnorm_g[l]))
                       * jnp.max(jnp.abs(diff_knorm_g[l]))).astype(F32).reshape(1)
        lam_vecs = jnp.stack([lambda_q1[l], lambda_k1[l], lambda_q2[l], lambda_k2[l]]).astype(F32)
        wr = w_br_ret[l].astype(BF16)
        wdf = w_br_diff[l].astype(BF16)
        wo = w_out[l].astype(BF16)
        wg = w_ffn_gate[l].astype(BF16)
        wu = w_ffn_up[l].astype(BF16)
        wd = w_ffn_down[l].astype(BF16)

        z16, k32, v32 = _in_projection(xp, g_mix, w_in_l, tabs_p, qg, kg, bd, T)
        y_r, s_p = _retention_prompt(z16, lg_tab, gn_r, jnp.zeros((B, H_RET, DK_RET, DV_RET), F32), B, T)
        y_d = _diff_prompt(z16, logit_bound, lam_vecs, gn_d, B, T, lam_init)
        x1, h2 = _merge(xp, y_r, y_d, z16, wr, wdf, wo, g_ffn)
        xp = _ffn(x1, h2, wg, wu, wd)
        outs[0].append(k32.reshape(B, T, H_DIFF, 2 * D_HEAD))
        outs[1].append(v32.reshape(B, T, H_DIFF, DV_DIFF))
        outs[2].append(s_p)

        z16, k32, v32 = _in_projection(xs, g_mix, w_in_l, tabs_s, qg, kg, bd, Bs * Ts)
        y_r, s_s = _retention_sample(z16, lg_tab, gn_r, state_ret[l], Bs, Ts)
        y_d = _diff_sample(z16, ck, cv, page_table + l * n_phys, lam_vecs, gn_d, Bs, Ts, lam_init)
        x1, h2 = _merge(xs, y_r, y_d, z16, wr, wdf, wo, g_ffn)
        xs = _ffn(x1, h2, wg, wu, wd)
        outs[3].append(k32.reshape(Bs, Ts, H_DIFF, 2 * D_HEAD))
        outs[4].append(v32.reshape(Bs, Ts, H_DIFF, DV_DIFF))
        outs[5].append(s_s)

    stacked = [jnp.stack(o) for o in outs]
    return (xp.reshape(B, T, D), xs.reshape(Bs, Ts, D),
            stacked[0], stacked[1], stacked[2], stacked[3], stacked[4], stacked[5])
```

```python
import functools
import math

import jax
import jax.numpy as jnp
from jax import lax
from jax.experimental import pallas as pl
from jax.experimental.pallas import tpu as pltpu

F32 = jnp.float32
BF16 = jnp.bfloat16

D_MODEL = 1024
H_RET = 4
DK_RET = 128
DV_RET = 256
H_DIFF = 8
D_HEAD = 64
DV_DIFF = 2 * D_HEAD
ROPE_THETA = 10000.0
EPS = 1e-6
LOG2E = 1.4426950408889634
SAFE_LOGIT_BOUND = 64.0

LANES = 128
SECTION = 1024
N_SECTIONS = 8
NORM_GROUP_TILE = 256
VMEM_LIMIT = 56 * 1024 * 1024

TOKEN_TILE = 512
RET_CHUNK = 512
ATTN_TILE = 512
ATTN_KV_TILE = 512
PAGES_PER_STEP = 8
RET_SAMPLE_GROUP = 8

NT_DIMS = (((1,), (1,)), ((), ()))
TN_DIMS = (((0,), (0,)), ((), ()))


def _sigmoid(x):
    return 1.0 / (1.0 + jnp.exp(-x))


def _params(semantics):
    return pltpu.CompilerParams(dimension_semantics=semantics, vmem_limit_bytes=VMEM_LIMIT)


def _inproj_kernel(x_ref, g_ref, w_ref, cr_ref, sr_ref, cd_ref, sd_ref, qg_ref, kg_ref, bd_ref,
                   z_ref, k32_ref, v32_ref):
    x = x_ref[...]
    ms = jnp.mean(x * x, axis=-1, keepdims=True)
    h = (x * lax.rsqrt(ms + EPS) * g_ref[...]).astype(BF16)

    def project(j):
        return jnp.dot(h, w_ref[:, j * SECTION:(j + 1) * SECTION], preferred_element_type=F32)

    def put(j, col, width, val):
        z_ref[:, j * SECTION + col:j * SECTION + col + width] = val.astype(BF16)

    acc = project(0)
    c = cr_ref[...]
    s = sr_ref[...]
    for hh in range(SECTION // LANES):
        blk = acc[:, hh * LANES:(hh + 1) * LANES]
        r = blk * c + pltpu.roll(blk, DK_RET // 2, 1) * s
        if hh >= H_RET:
            r = r * (DK_RET ** -0.5)
        put(0, hh * LANES, LANES, r)

    put(1, 0, SECTION, project(1))
    acc = project(2)
    put(2, 0, SECTION, acc * _sigmoid(acc))

    def qk_norm_rope(acc, gain, scale, emit):
        c = cd_ref[...]
        s = sd_ref[...]
        lane = lax.broadcasted_iota(jnp.int32, c.shape, 1)
        first_half = (lane % D_HEAD) < (D_HEAD // 2)
        for t in range(SECTION // NORM_GROUP_TILE):
            blk = acc[:, t * NORM_GROUP_TILE:(t + 1) * NORM_GROUP_TILE]
            ms = jnp.dot((blk * blk).astype(BF16), bd_ref[...], preferred_element_type=F32) * (1.0 / D_HEAD)
            y = blk * lax.rsqrt(ms + EPS)
            for u in range(NORM_GROUP_TILE // LANES):
                yb = y[:, u * LANES:(u + 1) * LANES] * gain
                partner = jnp.where(first_half,
                                    pltpu.roll(yb, LANES - D_HEAD // 2, 1),
                                    pltpu.roll(yb, D_HEAD // 2, 1))
                r = yb * c + partner * s
                if scale != 1.0:
                    r = r * scale
                emit(t * NORM_GROUP_TILE + u * LANES, r)

    qk_norm_rope(project(3), qg_ref[...], D_HEAD ** -0.5 * LOG2E, lambda col, r: put(3, col, LANES, r))

    def emit_k(col, r):
        k32_ref[:, col // LANES, :] = r
        put(4, col, LANES, r)
    qk_norm_rope(project(4), kg_ref[...], 1.0, emit_k)

    acc = project(5)
    for hh in range(H_DIFF):
        v32_ref[:, hh, :] = acc[:, hh * DV_DIFF:(hh + 1) * DV_DIFF]
    put(5, 0, SECTION, acc)

    for j in (6, 7):
        put(j, 0, SECTION, _sigmoid(project(j)))


def _in_projection(x, g, w_bf16, tabs, qg, kg, bd, table_rows):
    n = x.shape[0]
    tm = min(TOKEN_TILE, n)
    n_tab_blocks = table_rows // tm
    tab_spec = pl.BlockSpec((tm, LANES), lambda i: (i % n_tab_blocks, 0))
    const = lambda shape: pl.BlockSpec(shape, lambda i: (0, 0))
    return pl.pallas_call(
        _inproj_kernel,
        grid=(n // tm,),
        in_specs=[
            pl.BlockSpec((tm, D_MODEL), lambda i: (i, 0)),
            const((1, D_MODEL)),
            pl.BlockSpec((D_MODEL, N_SECTIONS * SECTION), lambda i: (0, 0), pipeline_mode=pl.Buffered(1)),
            tab_spec, tab_spec, tab_spec, tab_spec,
            const((1, LANES)), const((1, LANES)),
            const((NORM_GROUP_TILE, NORM_GROUP_TILE)),
        ],
        out_specs=[
            pl.BlockSpec((tm, N_SECTIONS * SECTION), lambda i: (i, 0)),
            pl.BlockSpec((tm, H_DIFF, DV_DIFF), lambda i: (i, 0, 0)),
            pl.BlockSpec((tm, H_DIFF, DV_DIFF), lambda i: (i, 0, 0)),
        ],
        out_shape=[
            jax.ShapeDtypeStruct((n, N_SECTIONS * SECTION), BF16),
            jax.ShapeDtypeStruct((n, H_DIFF, DV_DIFF), F32),
            jax.ShapeDtypeStruct((n, H_DIFF, DV_DIFF), F32),
        ],
        compiler_params=_params(("parallel",)),
        name="in_projection",
    )(x, g, w_bf16, *tabs, qg, kg, bd)


def _retention_kernel(lg_ref, q_ref, k_ref, v_ref, sg_ref, gn_ref, s0_ref, y_ref, s_ref, decay_ref):
    c = pl.program_id(2)
    chunk = q_ref.shape[0]
    lg = lg_ref[...][:, :1]

    @pl.when(c == 0)
    def _():
        s_ref[...] = s0_ref[...]
        row = lax.broadcasted_iota(jnp.int32, (chunk, chunk), 0)
        col = lax.broadcasted_iota(jnp.int32, (chunk, chunk), 1)
        dist = (row - col).astype(F32)
        decay_ref[...] = jnp.where(dist >= 0, jnp.exp(jnp.maximum(dist, 0.0) * lg), 0.0)

    idx = lax.broadcasted_iota(jnp.int32, (chunk, 1), 0).astype(F32)
    xi = jnp.exp((idx + 1.0) * lg)
    zeta = jnp.exp((chunk - 1.0 - idx) * lg)
    chunk_decay = jnp.exp(chunk * lg)

    q = q_ref[...]
    k = k_ref[...]
    v = v_ref[...]
    state = s_ref[...]
    att = lax.dot_general(q, k, NT_DIMS, preferred_element_type=F32) * decay_ref[...]
    o = (jnp.dot(att.astype(BF16), v, preferred_element_type=F32)
         + jnp.dot(q, state.astype(BF16), preferred_element_type=F32) * xi)
    kz = (k.astype(F32) * zeta).astype(BF16)
    s_ref[...] = chunk_decay * state + lax.dot_general(kz, v, TN_DIMS, preferred_element_type=F32)

    ms = jnp.mean(o * o, axis=-1, keepdims=True)
    y = o * lax.rsqrt(ms + EPS) * gn_ref[...] * sg_ref[...].astype(F32)
    y_ref[...] = y.astype(BF16)


def _retention_prompt(z16, lg_tab, gn, s0, batch, seq):
    chunk = min(RET_CHUNK, seq)
    n_chunks = seq // chunk
    row = lambda b, h, c: b * n_chunks + c
    return pl.pallas_call(
        _retention_kernel,
        grid=(batch, H_RET, n_chunks),
        in_specs=[
            pl.BlockSpec((None, 1, LANES), lambda b, h, c: (h, 0, 0)),
            pl.BlockSpec((chunk, DK_RET), lambda b, h, c: (row(b, h, c), h)),
            pl.BlockSpec((chunk, DK_RET), lambda b, h, c: (row(b, h, c), H_RET + h)),
            pl.BlockSpec((chunk, DV_RET), lambda b, h, c: (row(b, h, c), SECTION // DV_RET + h)),
            pl.BlockSpec((chunk, DV_RET), lambda b, h, c: (row(b, h, c), 2 * SECTION // DV_RET + h)),
            pl.BlockSpec((None, 1, DV_RET), lambda b, h, c: (h, 0, 0)),
            pl.BlockSpec((None, None, DK_RET, DV_RET), lambda b, h, c: (b, h, 0, 0)),
        ],
        out_specs=[
            pl.BlockSpec((chunk, DV_RET), lambda b, h, c: (row(b, h, c), h)),
            pl.BlockSpec((None, None, DK_RET, DV_RET), lambda b, h, c: (b, h, 0, 0)),
        ],
        out_shape=[
            jax.ShapeDtypeStruct((batch * seq, H_RET * DV_RET), BF16),
            jax.ShapeDtypeStruct((batch, H_RET, DK_RET, DV_RET), F32),
        ],
        scratch_shapes=[pltpu.VMEM((chunk, chunk), F32)],
        compiler_params=_params(("parallel", "parallel", "arbitrary")),
        name="retention_prompt",
    )(lg_tab, z16, z16, z16, z16, gn, s0)


def _retention_sample_kernel(lg_ref, q_ref, k_ref, v_ref, sg_ref, gn_ref, s0_ref, y_ref, s_ref, *, ts):
    rows = q_ref.shape[0]
    group = rows // ts
    row = lax.broadcasted_iota(jnp.int32, (rows, rows), 0)
    col = lax.broadcasted_iota(jnp.int32, (rows, rows), 1)
    same_seq = (row // ts) == (col // ts)
    dist = ((row % ts) - (col % ts)).astype(F32)
    ridx = lax.broadcasted_iota(jnp.int32, (rows, 1), 0)
    tpos = (ridx % ts).astype(F32)
    seq_of_row = ridx // ts

    for h in range(H_RET):
        lg = lg_ref[h][:, :1]
        decay = jnp.where(same_seq & (dist >= 0), jnp.exp(jnp.maximum(dist, 0.0) * lg), 0.0)
        xi = jnp.exp((tpos + 1.0) * lg)
        zeta = jnp.exp((ts - 1.0 - tpos) * lg)
        seq_decay = jnp.exp(ts * lg)
        q = q_ref[:, h * DK_RET:(h + 1) * DK_RET]
        k = k_ref[:, h * DK_RET:(h + 1) * DK_RET]
        v = v_ref[:, h * DV_RET:(h + 1) * DV_RET]
        att = lax.dot_general(q, k, NT_DIMS, preferred_element_type=F32) * decay
        o = jnp.dot(att.astype(BF16), v, preferred_element_type=F32)
        kz = k.astype(F32) * zeta
        for bb in range(group):
            mine = seq_of_row == bb
            state = s0_ref[bb, h]
            full = jnp.dot(q, state.astype(BF16), preferred_element_type=F32) * xi
            o = o + jnp.where(mine, full, 0.0)
            kz_b = jnp.where(mine, kz, 0.0).astype(BF16)
            s_ref[bb, h] = seq_decay * state + lax.dot_general(kz_b, v, TN_DIMS, preferred_element_type=F32)
        ms = jnp.mean(o * o, axis=-1, keepdims=True)
        y = o * lax.rsqrt(ms + EPS) * gn_ref[h] * sg_ref[:, h * DV_RET:(h + 1) * DV_RET].astype(F32)
        y_ref[:, h * DV_RET:(h + 1) * DV_RET] = y.astype(BF16)


def _retention_sample(z16, lg_tab, gn, s0, batch, ts):
    group = min(RET_SAMPLE_GROUP, batch)
    rows = group * ts
    return pl.pallas_call(
        functools.partial(_retention_sample_kernel, ts=ts),
        grid=(batch // group,),
        in_specs=[
            pl.BlockSpec((H_RET, 1, LANES), lambda i: (0, 0, 0)),
            pl.BlockSpec((rows, H_RET * DK_RET), lambda i: (i, 0)),
            pl.BlockSpec((rows, H_RET * DK_RET), lambda i: (i, 1)),
            pl.BlockSpec((rows, H_RET * DV_RET), lambda i: (i, 1)),
            pl.BlockSpec((rows, H_RET * DV_RET), lambda i: (i, 2)),
            pl.BlockSpec((H_RET, 1, DV_RET), lambda i: (0, 0, 0)),
            pl.BlockSpec((group, H_RET, DK_RET, DV_RET), lambda i: (i, 0, 0, 0)),
        ],
        out_specs=[
            pl.BlockSpec((rows, H_RET * DV_RET), lambda i: (i, 0)),
            pl.BlockSpec((group, H_RET, DK_RET, DV_RET), lambda i: (i, 0, 0, 0)),
        ],
        out_shape=[
            jax.ShapeDtypeStruct((batch * ts, H_RET * DV_RET), BF16),
            jax.ShapeDtypeStruct((batch, H_RET, DK_RET, DV_RET), F32),
        ],
        compiler_params=_params(("parallel",)),
        name="retention_sample",
    )(lg_tab, z16, z16, z16, z16, gn, s0)


def _lambda_value(lam_ref, lam_init):
    lv = lam_ref[...]
    a = jnp.sum(lv[0:1] * lv[1:2], axis=-1, keepdims=True)
    b = jnp.sum(lv[2:3] * lv[3:4], axis=-1, keepdims=True)
    return jnp.exp(a) - jnp.exp(b) + lam_init


def _head_norm(o, gain, lam_init):
    ms = jnp.mean(o * o, axis=-1, keepdims=True)
    return o * lax.rsqrt(ms + EPS) * gain * (1.0 - lam_init)


def _diff_prompt_kernel(bound_ref, lam_ref, q_ref, k_ref, v_ref, gn_ref, y_ref, acc_ref, *, lam_init, tk):
    qi = pl.program_id(2)
    tq = q_ref.shape[0]
    rows = 2 * tq
    q = q_ref[...]
    lane = lax.broadcasted_iota(jnp.int32, q.shape, 1)
    zero = jnp.zeros_like(q)
    qs = jnp.concatenate([jnp.where(lane < D_HEAD, q, zero), jnp.where(lane >= D_HEAD, q, zero)], axis=0)
    ones = jnp.ones((tk, LANES), BF16)
    acc_ref[...] = jnp.zeros(acc_ref.shape, F32)

    def lanes(x, n):
        return jnp.concatenate([x] * (n // LANES), axis=1)

    def scores(kb):
        k = k_ref[pl.ds(pl.multiple_of(kb * tk, tk), tk), :]
        return lax.dot_general(qs, k, NT_DIMS, preferred_element_type=F32)

    def values(kb):
        v = v_ref[pl.ds(pl.multiple_of(kb * tk, tk), tk), :]
        return jnp.concatenate([v, ones], axis=1)

    def causal(s):
        qpos = qi * tq + lax.broadcasted_iota(jnp.int32, s.shape, 0) % tq
        kpos = n_full * tk + lax.broadcasted_iota(jnp.int32, s.shape, 1)
        return jnp.where(kpos <= qpos, s, -jnp.inf)

    n_full = (qi * tq) // tk

    unstabilised = bound_ref[0] <= SAFE_LOGIT_BOUND

    def weights(kb, masked):
        s = scores(kb)
        return jnp.exp2(causal(s) if masked else s).astype(BF16)

    def accumulate(kb, p):
        acc_ref[...] += jnp.dot(p, values(kb), preferred_element_type=F32)

    @pl.when(unstabilised & (n_full == 0))
    def _():
        accumulate(0, weights(0, True))

    @pl.when(unstabilised & (n_full > 0))
    def _():
        def body(kb, p):
            p_next = weights(kb + 1, False)
            accumulate(kb, p)
            return p_next

        p = lax.fori_loop(0, n_full - 1, body, weights(0, False))
        p_last = weights(n_full, True)
        accumulate(n_full - 1, p)
        accumulate(n_full, p_last)

    @pl.when(bound_ref[0] > SAFE_LOGIT_BOUND)
    def _():
        def update(kb, s, m_prev):
            m_blk = jnp.max(s, axis=-1, keepdims=True)
            m_new = jnp.maximum(m_prev, jnp.broadcast_to(m_blk, (rows, LANES)))
            alpha = jnp.exp2(m_prev - m_new)
            p = jnp.exp2(s - lanes(m_new, tk)).astype(BF16)
            pv = jnp.dot(p, values(kb), preferred_element_type=F32)
            acc_ref[...] = lanes(alpha, 2 * LANES) * acc_ref[...] + pv
            return m_new

        m = lax.fori_loop(0, n_full, lambda kb, m_prev: update(kb, scores(kb), m_prev),
                          jnp.full((rows, LANES), -jnp.inf, F32))
        update(n_full, causal(scores(n_full)), m)

    acc = acc_ref[...]
    on = acc[:, :DV_DIFF] / acc[:, DV_DIFF:]
    lam = _lambda_value(lam_ref, lam_init)
    o = on[:tq] - lam * on[tq:]
    y_ref[...] = _head_norm(o, gn_ref[...], lam_init).astype(BF16)


def _diff_prompt(z16, logit_bound, lam_vecs, gn, batch, seq, lam_init):
    tq = min(ATTN_TILE, seq)
    tk = min(ATTN_KV_TILE, seq)
    nq = seq // tq
    q_col = 3 * SECTION // LANES
    k_col = 4 * SECTION // LANES
    v_col = 5 * SECTION // LANES
    return pl.pallas_call(
        functools.partial(_diff_prompt_kernel, lam_init=lam_init, tk=tk),
        grid=(batch, H_DIFF, nq),
        in_specs=[
            pl.BlockSpec(memory_space=pltpu.SMEM),
            pl.BlockSpec((4, D_HEAD), lambda b, h, i: (0, 0)),
            pl.BlockSpec((tq, LANES), lambda b, h, i: (b * nq + i, q_col + h)),
            pl.BlockSpec((seq, LANES), lambda b, h, i: (b, k_col + h)),
            pl.BlockSpec((seq, LANES), lambda b, h, i: (b, v_col + h)),
            pl.BlockSpec((1, DV_DIFF), lambda b, h, i: (0, 0)),
        ],
        out_specs=pl.BlockSpec((tq, DV_DIFF), lambda b, h, i: (b * nq + i, h)),
        out_shape=jax.ShapeDtypeStruct((batch * seq, H_DIFF * DV_DIFF), BF16),
        scratch_shapes=[pltpu.VMEM((2 * tq, 2 * DV_DIFF), F32)],
        compiler_params=_params(("parallel", "parallel", "arbitrary")),
        name="diff_attention_prompt",
    )(logit_bound, lam_vecs, z16, z16, z16, gn)


def _diff_sample_kernel(pt_ref, bound_ref, lam_ref, q_ref, kn_ref, vn_ref, gn_ref, *rest, ts, lam_init):
    del pt_ref
    pages = PAGES_PER_STEP
    k_refs = rest[:pages]
    v_refs = rest[pages:2 * pages]
    y_ref, q_scr, bias_scr, m_ref, l_ref, acc_ref = rest[2 * pages:]
    p_step = pl.program_id(1)
    hr = 2 * ts
    rows = H_DIFF * hr
    unstabilised = bound_ref[0] <= SAFE_LOGIT_BOUND

    def head_cols(x, h):
        return x[:, h * DV_DIFF:(h + 1) * DV_DIFF]

    @pl.when(p_step == 0)
    def _():
        q32 = q_ref[...].astype(F32)
        r8 = lax.broadcasted_iota(jnp.int32, (hr, LANES), 0)
        lane8 = lax.broadcasted_iota(jnp.int32, (hr, LANES), 1)
        keep = (lane8 // D_HEAD) == (r8 // ts)
        q_heads = []
        for h in range(H_DIFF):
            blk = head_cols(q32, h)
            qt = jnp.zeros((hr, LANES), F32)
            for t in range(ts):
                qt = jnp.where(r8 % ts == t, blk[t:t + 1, :], qt)
            q_heads.append(jnp.where(keep, qt, 0.0))
        q_all = jnp.concatenate(q_heads, axis=0)
        q_scr[...] = q_all
        brow = lax.broadcasted_iota(jnp.int32, bias_scr.shape, 0)
        bcol = lax.broadcasted_iota(jnp.int32, bias_scr.shape, 1)
        bias_scr[...] = jnp.where(bcol % H_DIFF == brow // hr, 0.0, -jnp.inf)

        kn = kn_ref[...].astype(F32)
        vn = vn_ref[...].astype(F32)

        def per_row(x, t):
            return jnp.concatenate(
                [jnp.broadcast_to(head_cols(x, h)[t:t + 1, :], (hr, LANES)) for h in range(H_DIFF)], axis=0)

        qpos = lax.broadcasted_iota(jnp.int32, (rows, 1), 0) % ts
        s_new = [jnp.sum(q_all * per_row(kn, t), axis=-1, keepdims=True) for t in range(ts)]
        m0 = s_new[0]
        for t in range(1, ts):
            m0 = jnp.where(qpos >= t, jnp.maximum(m0, s_new[t]), m0)
        m0 = jnp.where(unstabilised, 0.0, m0)
        l0 = jnp.zeros((rows, 1), F32)
        a0 = jnp.zeros((rows, LANES), F32)
        for t in range(ts):
            pt = jnp.where(qpos >= t, jnp.exp2(s_new[t] - m0), 0.0)
            l0 = l0 + pt
            a0 = a0 + pt * per_row(vn, t)
        m_ref[...] = m0
        l_ref[...] = l0
        acc_ref[...] = a0

    qb = q_scr[...].astype(BF16)
    bias = bias_scr[...]
    cols = bias.shape[1]

    def page_scores(r):
        return lax.dot_general(qb, k_refs[r][...].astype(BF16), NT_DIMS, preferred_element_type=F32) + bias

    def page_values(r):
        return v_refs[r][...].astype(BF16)

    @pl.when(unstabilised)
    def _():
        pv = jnp.zeros((rows, DV_DIFF), F32)
        psum = jnp.zeros((rows, LANES), F32)
        for r in range(pages):
            p = jnp.exp2(page_scores(r))
            for c0 in range(0, cols, LANES):
                psum = psum + p[:, c0:c0 + LANES]
            pv = pv + jnp.dot(p.astype(BF16), page_values(r), preferred_element_type=F32)
        l_ref[...] += jnp.sum(psum, axis=-1, keepdims=True)
        acc_ref[...] += pv

    @pl.when(jnp.logical_not(unstabilised))
    def _():
        s = jnp.concatenate([page_scores(r) for r in range(pages)], axis=1)
        m_prev = m_ref[...]
        m_new = jnp.maximum(m_prev, jnp.max(s, axis=-1, keepdims=True))
        alpha = jnp.exp2(m_prev - m_new)
        p = jnp.exp2(s - m_new)
        l_ref[...] = alpha * l_ref[...] + jnp.sum(p, axis=-1, keepdims=True)
        pb = p.astype(BF16)
        pv = jnp.dot(pb[:, :cols], page_values(0), preferred_element_type=F32)
        for r in range(1, pages):
            pv = pv + jnp.dot(pb[:, r * cols:(r + 1) * cols], page_values(r), preferred_element_type=F32)
        acc_ref[...] = alpha * acc_ref[...] + pv
        m_ref[...] = m_new

    @pl.when(p_step == pl.num_programs(1) - 1)
    def _():
        on = acc_ref[...] / l_ref[...]
        lam = _lambda_value(lam_ref, lam_init)
        gain = gn_ref[...]
        for h in range(H_DIFF):
            o = on[h * hr:h * hr + ts] - lam * on[h * hr + ts:(h + 1) * hr]
            y_ref[:, h * DV_DIFF:(h + 1) * DV_DIFF] = _head_norm(o, gain, lam_init).astype(BF16)


def _diff_sample(z16, cache_k, cache_v, page_table, logit_bound, lam_vecs, gn, batch, ts, lam_init):
    n_pages = page_table.shape[1]
    page_rows = cache_k.shape[1]
    width = H_DIFF * DV_DIFF
    pages = PAGES_PER_STEP
    assert n_pages % pages == 0
    rows = 2 * H_DIFF * ts
    z3 = z16.reshape(batch, ts, N_SECTIONS * SECTION)
    new_spec = lambda sec: pl.BlockSpec((None, ts, SECTION), lambda b, p, pt: (b, 0, sec))

    def page_spec(r):
        return pl.BlockSpec((None, page_rows, DV_DIFF), lambda b, p, pt: (pt[b, p * pages + r], 0, 0))

    grid_spec = pltpu.PrefetchScalarGridSpec(
        num_scalar_prefetch=1,
        grid=(batch, n_pages // pages),
        in_specs=[
            pl.BlockSpec(memory_space=pltpu.SMEM),
            pl.BlockSpec((4, D_HEAD), lambda b, p, pt: (0, 0)),
            new_spec(3), new_spec(4), new_spec(5),
            pl.BlockSpec((1, DV_DIFF), lambda b, p, pt: (0, 0)),
        ] + [page_spec(r) for r in range(pages)] + [page_spec(r) for r in range(pages)],
        out_specs=pl.BlockSpec((None, ts, width), lambda b, p, pt: (b, 0, 0)),
        scratch_shapes=[
            pltpu.VMEM((rows, LANES), F32),
            pltpu.VMEM((rows, page_rows), F32),
            pltpu.VMEM((rows, 1), F32),
            pltpu.VMEM((rows, 1), F32),
            pltpu.VMEM((rows, DV_DIFF), F32),
        ],
    )
    y = pl.pallas_call(
        functools.partial(_diff_sample_kernel, ts=ts, lam_init=lam_init),
        grid_spec=grid_spec,
        out_shape=jax.ShapeDtypeStruct((batch, ts, width), BF16),
        compiler_params=_params(("parallel", "arbitrary")),
        name="diff_attention_sample",
    )(page_table, logit_bound, lam_vecs, z3, z3, z3, gn, *([cache_k] * pages), *([cache_v] * pages))
    return y.reshape(batch * ts, width)


def _merge_kernel(x_ref, yr_ref, yd_ref, gr_ref, gd_ref, wr_ref, wd_ref, wo_ref, g_ref, x1_ref, h_ref):
    a = jnp.dot(yr_ref[...], wr_ref[...], preferred_element_type=F32)
    b = jnp.dot(yd_ref[...], wd_ref[...], preferred_element_type=F32)
    m = gr_ref[...].astype(F32) * a + gd_ref[...].astype(F32) * b
    x1 = x_ref[...] + jnp.dot(m.astype(BF16), wo_ref[...], preferred_element_type=F32)
    x1_ref[...] = x1
    ms = jnp.mean(x1 * x1, axis=-1, keepdims=True)
    h_ref[...] = (x1 * lax.rsqrt(ms + EPS) * g_ref[...]).astype(BF16)


def _merge(x, y_r, y_d, z16, w_br_ret, w_br_diff, w_out, g_ffn):
    n = x.shape[0]
    tm = min(TOKEN_TILE, n)
    tile = lambda col: pl.BlockSpec((tm, D_MODEL), lambda i: (i, col))
    weight = pl.BlockSpec((D_MODEL, D_MODEL), lambda i: (0, 0))
    return pl.pallas_call(
        _merge_kernel,
        grid=(n // tm,),
        in_specs=[tile(0), tile(0), tile(0), tile(6), tile(7), weight, weight, weight,
                  pl.BlockSpec((1, D_MODEL), lambda i: (0, 0))],
        out_specs=[tile(0), tile(0)],
        out_shape=[jax.ShapeDtypeStruct((n, D_MODEL), F32), jax.ShapeDtypeStruct((n, D_MODEL), BF16)],
        compiler_params=_params(("parallel",)),
        name="merge",
    )(x, y_r, y_d, z16, z16, w_br_ret, w_br_diff, w_out, g_ffn)


def _ffn_chunks(d_ff):
    edges = list(range(0, d_ff, SECTION)) + [d_ff]
    return list(zip(edges[:-1], edges[1:]))


def _ffn_kernel(x_ref, h_ref, wg_ref, wu_ref, wd_ref, o_ref):
    h = h_ref[...]
    acc = x_ref[...]
    for c0, c1 in _ffn_chunks(wg_ref.shape[1]):
        g = jnp.dot(h, wg_ref[:, c0:c1], preferred_element_type=F32)
        u = jnp.dot(h, wu_ref[:, c0:c1], preferred_element_type=F32)
        a = (g * _sigmoid(g) * u).astype(BF16)
        acc = acc + jnp.dot(a, wd_ref[c0:c1, :], preferred_element_type=F32)
    o_ref[...] = acc


def _ffn(x1, h, wg, wu, wd):
    n = x1.shape[0]
    d_ff = wg.shape[1]
    tm = min(TOKEN_TILE, n)
    tile = pl.BlockSpec((tm, D_MODEL), lambda i: (i, 0))
    single = pl.Buffered(1)
    return pl.pallas_call(
        _ffn_kernel,
        grid=(n // tm,),
        in_specs=[tile, tile,
                  pl.BlockSpec((D_MODEL, d_ff), lambda i: (0, 0), pipeline_mode=single),
                  pl.BlockSpec((D_MODEL, d_ff), lambda i: (0, 0), pipeline_mode=single),
                  pl.BlockSpec((d_ff, D_MODEL), lambda i: (0, 0), pipeline_mode=single)],
        out_specs=tile,
        out_shape=jax.ShapeDtypeStruct((n, D_MODEL), F32),
        compiler_params=_params(("parallel",)),
        name="ffn",
    )(x1, h, wg, wu, wd)


def _rope_tables(pos, half):
    inv = 1.0 / jnp.power(ROPE_THETA, jnp.arange(half, dtype=F32) / half)
    ang = pos.astype(F32)[:, None] * inv[None, :]
    cos = jnp.cos(ang)
    sin = jnp.sin(ang)
    reps = LANES // (2 * half)
    return (jnp.tile(jnp.concatenate([cos, cos], axis=-1), (1, reps)),
            jnp.tile(jnp.concatenate([-sin, sin], axis=-1), (1, reps)))


def _layer_tables(pos):
    cr, sr = _rope_tables(pos, DK_RET // 2)
    cd, sd = _rope_tables(pos, D_HEAD // 2)
    return (cr, sr, cd, sd)


def kernel(x_prompt, x_sample, cache_k, cache_v, state_ret, page_table, norm_mix_g, w_in, ret_gn_g, diff_qnorm_g, diff_knorm_g, lambda_q1, lambda_k1, lambda_q2, lambda_k2, diff_gn_g, w_br_ret, w_br_diff, w_out, norm_ffn_g, w_ffn_gate, w_ffn_up, w_ffn_down):
    B, T, D = x_prompt.shape
    Bs, Ts, _ = x_sample.shape
    depth, n_phys, page = cache_k.shape[:3]
    past_len = page_table.shape[1] * page

    tabs_p = _layer_tables(jnp.arange(T, dtype=jnp.int32))
    tabs_s = _layer_tables(jnp.tile(past_len + jnp.arange(Ts, dtype=jnp.int32), Bs))
    lg = jnp.log(1.0 - jnp.power(2.0, -5.0 - jnp.arange(H_RET, dtype=F32)))
    lg_tab = jnp.broadcast_to(lg[:, None, None], (H_RET, 1, LANES))
    gidx = jnp.arange(NORM_GROUP_TILE) // D_HEAD
    bd = (gidx[:, None] == gidx[None, :]).astype(BF16)
    ck = cache_k.reshape(depth * n_phys, page * H_DIFF, DV_DIFF)
    cv = cache_v.reshape(depth * n_phys, page * H_DIFF, DV_DIFF)

    xp = x_prompt.reshape(B * T, D)
    xs = x_sample.reshape(Bs * Ts, D)
    outs = [[] for _ in range(6)]
    for l in range(depth):
        lam_init = 0.8 - 0.6 * math.exp(-0.3 * l)
        w_in_l = w_in[l].astype(BF16)
        g_mix = norm_mix_g[l].reshape(1, D)
        g_ffn = norm_ffn_g[l].reshape(1, D)
        qg = jnp.tile(diff_qnorm_g[l], LANES // D_HEAD).reshape(1, LANES)
        kg = jnp.tile(diff_knorm_g[l], LANES // D_HEAD).reshape(1, LANES)
        gn_r = ret_gn_g[l].reshape(H_RET, 1, DV_RET)
        gn_d = diff_gn_g[l].reshape(1, DV_DIFF)
        logit_bound = (1.02 * LOG2E * D_HEAD ** 0.5 * jnp.max(jnp.abs(diff_qnorm_g[l]))
                       * jnp.max(jnp.abs(diff_knorm_g[l]))).astype(F32).reshape(1)
        lam_vecs = jnp.stack([lambda_q1[l], lambda_k1[l], lambda_q2[l], lambda_k2[l]]).astype(F32)
        wr = w_br_ret[l].astype(BF16)
        wdf = w_br_diff[l].astype(BF16)
        wo = w_out[l].astype(BF16)
        wg = w_ffn_gate[l].astype(BF16)
        wu = w_ffn_up[l].astype(BF16)
        wd = w_ffn_down[l].astype(BF16)

        z16, k32, v32 = _in_projection(xp, g_mix, w_in_l, tabs_p, qg, kg, bd, T)
        y_r, s_p = _retention_prompt(z16, lg_tab, gn_r, jnp.zeros((B, H_RET, DK_RET, DV_RET), F32), B, T)
        y_d = _diff_prompt(z16, logit_bound, lam_vecs, gn_d, B, T, lam_init)
        x1, h2 = _merge(xp, y_r, y_d, z16, wr, wdf, wo, g_ffn)
        xp = _ffn(x1, h2, wg, wu, wd)
        outs[0].append(k32.reshape(B, T, H_DIFF, 2 * D_HEAD))
        outs[1].append(v32.reshape(B, T, H_DIFF, DV_DIFF))
        outs[2].append(s_p)

        z16, k32, v32 = _in_projection(xs, g_mix, w_in_l, tabs_s, qg, kg, bd, Bs * Ts)
        y_r, s_s = _retention_sample(z16, lg_tab, gn_r, state_ret[l], Bs, Ts)
        y_d = _diff_sample(z16, ck, cv, page_table + l * n_phys, logit_bound, lam_vecs, gn_d, Bs, Ts, lam_init)
        x1, h2 = _merge(xs, y_r, y_d, z16, wr, wdf, wo, g_ffn)
        xs = _ffn(x1, h2, wg, wu, wd)
        outs[3].append(k32.reshape(Bs, Ts, H_DIFF, 2 * D_HEAD))
        outs[4].append(v32.reshape(Bs, Ts, H_DIFF, DV_DIFF))
        outs[5].append(s_s)

    stacked = [jnp.stack(o) for o in outs]
    return (xp.reshape(B, T, D), xs.reshape(Bs, Ts, D),
            stacked[0], stacked[1], stacked[2], stacked[3], stacked[4], stacked[5])
```

```python
import functools
import math

import jax
import jax.numpy as jnp
import numpy as np
from jax import lax
from jax.experimental import pallas as pl
from jax.experimental.pallas import tpu as pltpu

F32 = jnp.float32
BF16 = jnp.bfloat16

D_MODEL = 1024
H_RET = 4
DK_RET = 128
DV_RET = 256
H_DIFF = 8
D_HEAD = 64
DV_DIFF = 2 * D_HEAD
ROPE_THETA = 10000.0
EPS = 1e-6
LOG2E = 1.4426950408889634
SAFE_LOGIT_BOUND = 64.0

LANES = 128
SECTION = 1024
N_SECTIONS = 8
NORM_GROUP_TILE = 256
VMEM_LIMIT = 56 * 1024 * 1024

TOKEN_TILE = 512
RET_CHUNK = 512
ATTN_TILE = 512
ATTN_KV_TILE = 512
PAGES_PER_STEP = 8
RET_SAMPLE_GROUP = 8
ATTN_SLOTS = 3
SCHED_REC = 12
NEVER = -(1 << 30)

NT_DIMS = (((1,), (1,)), ((), ()))
TN_DIMS = (((0,), (0,)), ((), ()))


def _sigmoid(x):
    return 1.0 / (1.0 + jnp.exp(-x))


def _params(semantics):
    return pltpu.CompilerParams(dimension_semantics=semantics, vmem_limit_bytes=VMEM_LIMIT)


def _inproj_kernel(x_ref, g_ref, w_ref, cr_ref, sr_ref, cd_ref, sd_ref, qg_ref, kg_ref, bd_ref,
                   z_ref, k32_ref, v32_ref):
    x = x_ref[...]
    ms = jnp.mean(x * x, axis=-1, keepdims=True)
    h = (x * lax.rsqrt(ms + EPS) * g_ref[...]).astype(BF16)

    def project(j):
        return jnp.dot(h, w_ref[:, j * SECTION:(j + 1) * SECTION], preferred_element_type=F32)

    def put(j, col, width, val):
        z_ref[:, j * SECTION + col:j * SECTION + col + width] = val.astype(BF16)

    acc = project(0)
    c = cr_ref[...]
    s = sr_ref[...]
    for hh in range(SECTION // LANES):
        blk = acc[:, hh * LANES:(hh + 1) * LANES]
        r = blk * c + pltpu.roll(blk, DK_RET // 2, 1) * s
        if hh >= H_RET:
            r = r * (DK_RET ** -0.5)
        put(0, hh * LANES, LANES, r)

    put(1, 0, SECTION, project(1))
    acc = project(2)
    put(2, 0, SECTION, acc * _sigmoid(acc))

    def qk_norm_rope(acc, gain, scale, emit):
        c = cd_ref[...]
        s = sd_ref[...]
        lane = lax.broadcasted_iota(jnp.int32, c.shape, 1)
        first_half = (lane % D_HEAD) < (D_HEAD // 2)
        for t in range(SECTION // NORM_GROUP_TILE):
            blk = acc[:, t * NORM_GROUP_TILE:(t + 1) * NORM_GROUP_TILE]
            ms = jnp.dot((blk * blk).astype(BF16), bd_ref[...], preferred_element_type=F32) * (1.0 / D_HEAD)
            y = blk * lax.rsqrt(ms + EPS)
            for u in range(NORM_GROUP_TILE // LANES):
                yb = y[:, u * LANES:(u + 1) * LANES] * gain
                partner = jnp.where(first_half,
                                    pltpu.roll(yb, LANES - D_HEAD // 2, 1),
                                    pltpu.roll(yb, D_HEAD // 2, 1))
                r = yb * c + partner * s
                if scale != 1.0:
                    r = r * scale
                emit(t * NORM_GROUP_TILE + u * LANES, r)

    qk_norm_rope(project(3), qg_ref[...], D_HEAD ** -0.5 * LOG2E, lambda col, r: put(3, col, LANES, r))

    def emit_k(col, r):
        k32_ref[:, col // LANES, :] = r
        put(4, col, LANES, r)
    qk_norm_rope(project(4), kg_ref[...], 1.0, emit_k)

    acc = project(5)
    for hh in range(H_DIFF):
        v32_ref[:, hh, :] = acc[:, hh * DV_DIFF:(hh + 1) * DV_DIFF]
    put(5, 0, SECTION, acc)

    for j in (6, 7):
        put(j, 0, SECTION, _sigmoid(project(j)))


def _in_projection(x, g, w_bf16, tabs, qg, kg, bd, table_rows):
    n = x.shape[0]
    tm = min(TOKEN_TILE, n)
    n_tab_blocks = table_rows // tm
    tab_spec = pl.BlockSpec((tm, LANES), lambda i: (i % n_tab_blocks, 0))
    const = lambda shape: pl.BlockSpec(shape, lambda i: (0, 0))
    return pl.pallas_call(
        _inproj_kernel,
        grid=(n // tm,),
        in_specs=[
            pl.BlockSpec((tm, D_MODEL), lambda i: (i, 0)),
            const((1, D_MODEL)),
            pl.BlockSpec((D_MODEL, N_SECTIONS * SECTION), lambda i: (0, 0), pipeline_mode=pl.Buffered(1)),
            tab_spec, tab_spec, tab_spec, tab_spec,
            const((1, LANES)), const((1, LANES)),
            const((NORM_GROUP_TILE, NORM_GROUP_TILE)),
        ],
        out_specs=[
            pl.BlockSpec((tm, N_SECTIONS * SECTION), lambda i: (i, 0)),
            pl.BlockSpec((tm, H_DIFF, DV_DIFF), lambda i: (i, 0, 0)),
            pl.BlockSpec((tm, H_DIFF, DV_DIFF), lambda i: (i, 0, 0)),
        ],
        out_shape=[
            jax.ShapeDtypeStruct((n, N_SECTIONS * SECTION), BF16),
            jax.ShapeDtypeStruct((n, H_DIFF, DV_DIFF), F32),
            jax.ShapeDtypeStruct((n, H_DIFF, DV_DIFF), F32),
        ],
        compiler_params=_params(("parallel",)),
        name="in_projection",
    )(x, g, w_bf16, *tabs, qg, kg, bd)


def _retention_kernel(lg_ref, q_ref, k_ref, v_ref, sg_ref, gn_ref, s0_ref, y_ref, s_ref, decay_ref):
    c = pl.program_id(2)
    chunk = q_ref.shape[0]
    lg = lg_ref[...][:, :1]

    @pl.when(c == 0)
    def _():
        s_ref[...] = s0_ref[...]
        row = lax.broadcasted_iota(jnp.int32, (chunk, chunk), 0)
        col = lax.broadcasted_iota(jnp.int32, (chunk, chunk), 1)
        dist = (row - col).astype(F32)
        decay_ref[...] = jnp.where(dist >= 0, jnp.exp(jnp.maximum(dist, 0.0) * lg), 0.0)

    idx = lax.broadcasted_iota(jnp.int32, (chunk, 1), 0).astype(F32)
    xi = jnp.exp((idx + 1.0) * lg)
    zeta = jnp.exp((chunk - 1.0 - idx) * lg)
    chunk_decay = jnp.exp(chunk * lg)

    q = q_ref[...]
    k = k_ref[...]
    v = v_ref[...]
    state = s_ref[...]
    att = lax.dot_general(q, k, NT_DIMS, preferred_element_type=F32) * decay_ref[...]
    o = (jnp.dot(att.astype(BF16), v, preferred_element_type=F32)
         + jnp.dot(q, state.astype(BF16), preferred_element_type=F32) * xi)
    kz = (k.astype(F32) * zeta).astype(BF16)
    s_ref[...] = chunk_decay * state + lax.dot_general(kz, v, TN_DIMS, preferred_element_type=F32)

    ms = jnp.mean(o * o, axis=-1, keepdims=True)
    y = o * lax.rsqrt(ms + EPS) * gn_ref[...] * sg_ref[...].astype(F32)
    y_ref[...] = y.astype(BF16)


def _retention_prompt(z16, lg_tab, gn, s0, batch, seq):
    chunk = min(RET_CHUNK, seq)
    n_chunks = seq // chunk
    row = lambda b, h, c: b * n_chunks + c
    return pl.pallas_call(
        _retention_kernel,
        grid=(batch, H_RET, n_chunks),
        in_specs=[
            pl.BlockSpec((None, 1, LANES), lambda b, h, c: (h, 0, 0)),
            pl.BlockSpec((chunk, DK_RET), lambda b, h, c: (row(b, h, c), h)),
            pl.BlockSpec((chunk, DK_RET), lambda b, h, c: (row(b, h, c), H_RET + h)),
            pl.BlockSpec((chunk, DV_RET), lambda b, h, c: (row(b, h, c), SECTION // DV_RET + h)),
            pl.BlockSpec((chunk, DV_RET), lambda b, h, c: (row(b, h, c), 2 * SECTION // DV_RET + h)),
            pl.BlockSpec((None, 1, DV_RET), lambda b, h, c: (h, 0, 0)),
            pl.BlockSpec((None, None, DK_RET, DV_RET), lambda b, h, c: (b, h, 0, 0)),
        ],
        out_specs=[
            pl.BlockSpec((chunk, DV_RET), lambda b, h, c: (row(b, h, c), h)),
            pl.BlockSpec((None, None, DK_RET, DV_RET), lambda b, h, c: (b, h, 0, 0)),
        ],
        out_shape=[
            jax.ShapeDtypeStruct((batch * seq, H_RET * DV_RET), BF16),
            jax.ShapeDtypeStruct((batch, H_RET, DK_RET, DV_RET), F32),
        ],
        scratch_shapes=[pltpu.VMEM((chunk, chunk), F32)],
        compiler_params=_params(("parallel", "parallel", "arbitrary")),
        name="retention_prompt",
    )(lg_tab, z16, z16, z16, z16, gn, s0)


def _retention_sample_kernel(lg_ref, q_ref, k_ref, v_ref, sg_ref, gn_ref, s0_ref, y_ref, s_ref, *, ts):
    rows = q_ref.shape[0]
    group = rows // ts
    row = lax.broadcasted_iota(jnp.int32, (rows, rows), 0)
    col = lax.broadcasted_iota(jnp.int32, (rows, rows), 1)
    same_seq = (row // ts) == (col // ts)
    dist = ((row % ts) - (col % ts)).astype(F32)
    ridx = lax.broadcasted_iota(jnp.int32, (rows, 1), 0)
    tpos = (ridx % ts).astype(F32)
    seq_of_row = ridx // ts

    for h in range(H_RET):
        lg = lg_ref[h][:, :1]
        decay = jnp.where(same_seq & (dist >= 0), jnp.exp(jnp.maximum(dist, 0.0) * lg), 0.0)
        xi = jnp.exp((tpos + 1.0) * lg)
        zeta = jnp.exp((ts - 1.0 - tpos) * lg)
        seq_decay = jnp.exp(ts * lg)
        q = q_ref[:, h * DK_RET:(h + 1) * DK_RET]
        k = k_ref[:, h * DK_RET:(h + 1) * DK_RET]
        v = v_ref[:, h * DV_RET:(h + 1) * DV_RET]
        att = lax.dot_general(q, k, NT_DIMS, preferred_element_type=F32) * decay
        o = jnp.dot(att.astype(BF16), v, preferred_element_type=F32)
        kz = k.astype(F32) * zeta
        for bb in range(group):
            mine = seq_of_row == bb
            state = s0_ref[bb, h]
            full = jnp.dot(q, state.astype(BF16), preferred_element_type=F32) * xi
            o = o + jnp.where(mine, full, 0.0)
            kz_b = jnp.where(mine, kz, 0.0).astype(BF16)
            s_ref[bb, h] = seq_decay * state + lax.dot_general(kz_b, v, TN_DIMS, preferred_element_type=F32)
        ms = jnp.mean(o * o, axis=-1, keepdims=True)
        y = o * lax.rsqrt(ms + EPS) * gn_ref[h] * sg_ref[:, h * DV_RET:(h + 1) * DV_RET].astype(F32)
        y_ref[:, h * DV_RET:(h + 1) * DV_RET] = y.astype(BF16)


def _retention_sample(z16, lg_tab, gn, s0, batch, ts):
    group = min(RET_SAMPLE_GROUP, batch)
    rows = group * ts
    return pl.pallas_call(
        functools.partial(_retention_sample_kernel, ts=ts),
        grid=(batch // group,),
        in_specs=[
            pl.BlockSpec((H_RET, 1, LANES), lambda i: (0, 0, 0)),
            pl.BlockSpec((rows, H_RET * DK_RET), lambda i: (i, 0)),
            pl.BlockSpec((rows, H_RET * DK_RET), lambda i: (i, 1)),
            pl.BlockSpec((rows, H_RET * DV_RET), lambda i: (i, 1)),
            pl.BlockSpec((rows, H_RET * DV_RET), lambda i: (i, 2)),
            pl.BlockSpec((H_RET, 1, DV_RET), lambda i: (0, 0, 0)),
            pl.BlockSpec((group, H_RET, DK_RET, DV_RET), lambda i: (i, 0, 0, 0)),
        ],
        out_specs=[
            pl.BlockSpec((rows, H_RET * DV_RET), lambda i: (i, 0)),
            pl.BlockSpec((group, H_RET, DK_RET, DV_RET), lambda i: (i, 0, 0, 0)),
        ],
        out_shape=[
            jax.ShapeDtypeStruct((batch * ts, H_RET * DV_RET), BF16),
            jax.ShapeDtypeStruct((batch, H_RET, DK_RET, DV_RET), F32),
        ],
        compiler_params=_params(("parallel",)),
        name="retention_sample",
    )(lg_tab, z16, z16, z16, z16, gn, s0)


def _lambda_value(lam_ref, lam_init):
    lv = lam_ref[...]
    a = jnp.sum(lv[0:1] * lv[1:2], axis=-1, keepdims=True)
    b = jnp.sum(lv[2:3] * lv[3:4], axis=-1, keepdims=True)
    return jnp.exp(a) - jnp.exp(b) + lam_init


def _head_norm(o, gain, lam_init):
    ms = jnp.mean(o * o, axis=-1, keepdims=True)
    return o * lax.rsqrt(ms + EPS) * gain * (1.0 - lam_init)


def _diff_prompt_kernel(bound_ref, lam_ref, q_ref, k_ref, v_ref, gn_ref, y_ref, acc_ref, *, lam_init, tk):
    qi = pl.program_id(2)
    tq = q_ref.shape[0]
    rows = 2 * tq
    q = q_ref[...]
    lane = lax.broadcasted_iota(jnp.int32, q.shape, 1)
    zero = jnp.zeros_like(q)
    qs = jnp.concatenate([jnp.where(lane < D_HEAD, q, zero), jnp.where(lane >= D_HEAD, q, zero)], axis=0)
    ones = jnp.ones((tk, LANES), BF16)
    acc_ref[...] = jnp.zeros(acc_ref.shape, F32)

    def lanes(x, n):
        return jnp.concatenate([x] * (n // LANES), axis=1)

    def scores(kb):
        k = k_ref[pl.ds(pl.multiple_of(kb * tk, tk), tk), :]
        return lax.dot_general(qs, k, NT_DIMS, preferred_element_type=F32)

    def values(kb):
        v = v_ref[pl.ds(pl.multiple_of(kb * tk, tk), tk), :]
        return jnp.concatenate([v, ones], axis=1)

    def causal(s):
        qpos = qi * tq + lax.broadcasted_iota(jnp.int32, s.shape, 0) % tq
        kpos = n_full * tk + lax.broadcasted_iota(jnp.int32, s.shape, 1)
        return jnp.where(kpos <= qpos, s, -jnp.inf)

    n_full = (qi * tq) // tk

    unstabilised = bound_ref[0] <= SAFE_LOGIT_BOUND

    def weights(kb, masked):
        s = scores(kb)
        return jnp.exp2(causal(s) if masked else s).astype(BF16)

    def accumulate(kb, p):
        acc_ref[...] += jnp.dot(p, values(kb), preferred_element_type=F32)

    @pl.when(unstabilised & (n_full == 0))
    def _():
        accumulate(0, weights(0, True))

    @pl.when(unstabilised & (n_full > 0))
    def _():
        def body(kb, p):
            p_next = weights(kb + 1, False)
            accumulate(kb, p)
            return p_next

        p = lax.fori_loop(0, n_full - 1, body, weights(0, False))
        p_last = weights(n_full, True)
        accumulate(n_full - 1, p)
        accumulate(n_full, p_last)

    @pl.when(bound_ref[0] > SAFE_LOGIT_BOUND)
    def _():
        def update(kb, s, m_prev):
            m_blk = jnp.max(s, axis=-1, keepdims=True)
            m_new = jnp.maximum(m_prev, jnp.broadcast_to(m_blk, (rows, LANES)))
            alpha = jnp.exp2(m_prev - m_new)
            p = jnp.exp2(s - lanes(m_new, tk)).astype(BF16)
            pv = jnp.dot(p, values(kb), preferred_element_type=F32)
            acc_ref[...] = lanes(alpha, 2 * LANES) * acc_ref[...] + pv
            return m_new

        m = lax.fori_loop(0, n_full, lambda kb, m_prev: update(kb, scores(kb), m_prev),
                          jnp.full((rows, LANES), -jnp.inf, F32))
        update(n_full, causal(scores(n_full)), m)

    acc = acc_ref[...]
    on = acc[:, :DV_DIFF] / acc[:, DV_DIFF:]
    lam = _lambda_value(lam_ref, lam_init)
    o = on[:tq] - lam * on[tq:]
    y_ref[...] = _head_norm(o, gn_ref[...], lam_init).astype(BF16)


def _diff_prompt(z16, logit_bound, lam_vecs, gn, batch, seq, lam_init):
    tq = min(ATTN_TILE, seq)
    tk = min(ATTN_KV_TILE, seq)
    nq = seq // tq
    q_col = 3 * SECTION // LANES
    k_col = 4 * SECTION // LANES
    v_col = 5 * SECTION // LANES
    return pl.pallas_call(
        functools.partial(_diff_prompt_kernel, lam_init=lam_init, tk=tk),
        grid=(batch, H_DIFF, nq),
        in_specs=[
            pl.BlockSpec(memory_space=pltpu.SMEM),
            pl.BlockSpec((4, D_HEAD), lambda b, h, i: (0, 0)),
            pl.BlockSpec((tq, LANES), lambda b, h, i: (b * nq + i, q_col + h)),
            pl.BlockSpec((seq, LANES), lambda b, h, i: (b, k_col + h)),
            pl.BlockSpec((seq, LANES), lambda b, h, i: (b, v_col + h)),
            pl.BlockSpec((1, DV_DIFF), lambda b, h, i: (0, 0)),
        ],
        out_specs=pl.BlockSpec((tq, DV_DIFF), lambda b, h, i: (b * nq + i, h)),
        out_shape=jax.ShapeDtypeStruct((batch * seq, H_DIFF * DV_DIFF), BF16),
        scratch_shapes=[pltpu.VMEM((2 * tq, 2 * DV_DIFF), F32)],
        compiler_params=_params(("parallel", "parallel", "arbitrary")),
        name="diff_attention_prompt",
    )(logit_bound, lam_vecs, z16, z16, z16, gn)


def _diff_sample_kernel(pt_ref, bound_ref, lam_ref, q_ref, kn_ref, vn_ref, gn_ref, *rest, ts, lam_init):
    del pt_ref
    pages = PAGES_PER_STEP
    k_refs = rest[:pages]
    v_refs = rest[pages:2 * pages]
    y_ref, q_scr, bias_scr, m_ref, l_ref, acc_ref = rest[2 * pages:]
    p_step = pl.program_id(1)
    hr = 2 * ts
    rows = H_DIFF * hr
    unstabilised = bound_ref[0] <= SAFE_LOGIT_BOUND

    def head_cols(x, h):
        return x[:, h * DV_DIFF:(h + 1) * DV_DIFF]

    @pl.when(p_step == 0)
    def _():
        q32 = q_ref[...].astype(F32)
        r8 = lax.broadcasted_iota(jnp.int32, (hr, LANES), 0)
        lane8 = lax.broadcasted_iota(jnp.int32, (hr, LANES), 1)
        keep = (lane8 // D_HEAD) == (r8 // ts)
        q_heads = []
        for h in range(H_DIFF):
            blk = head_cols(q32, h)
            qt = jnp.zeros((hr, LANES), F32)
            for t in range(ts):
                qt = jnp.where(r8 % ts == t, blk[t:t + 1, :], qt)
            q_heads.append(jnp.where(keep, qt, 0.0))
        q_all = jnp.concatenate(q_heads, axis=0)
        q_scr[...] = q_all
        brow = lax.broadcasted_iota(jnp.int32, bias_scr.shape, 0)
        bcol = lax.broadcasted_iota(jnp.int32, bias_scr.shape, 1)
        bias_scr[...] = jnp.where(bcol % H_DIFF == brow // hr, 0.0, -jnp.inf)

        kn = kn_ref[...].astype(F32)
        vn = vn_ref[...].astype(F32)

        def per_row(x, t):
            return jnp.concatenate(
                [jnp.broadcast_to(head_cols(x, h)[t:t + 1, :], (hr, LANES)) for h in range(H_DIFF)], axis=0)

        qpos = lax.broadcasted_iota(jnp.int32, (rows, 1), 0) % ts
        s_new = [jnp.sum(q_all * per_row(kn, t), axis=-1, keepdims=True) for t in range(ts)]
        m0 = s_new[0]
        for t in range(1, ts):
            m0 = jnp.where(qpos >= t, jnp.maximum(m0, s_new[t]), m0)
        m0 = jnp.where(unstabilised, 0.0, m0)
        l0 = jnp.zeros((rows, 1), F32)
        a0 = jnp.zeros((rows, LANES), F32)
        for t in range(ts):
            pt = jnp.where(qpos >= t, jnp.exp2(s_new[t] - m0), 0.0)
            l0 = l0 + pt
            a0 = a0 + pt * per_row(vn, t)
        m_ref[...] = m0
        l_ref[...] = l0
        acc_ref[...] = a0

    qb = q_scr[...].astype(BF16)
    bias = bias_scr[...]
    cols = bias.shape[1]

    def page_scores(r):
        return lax.dot_general(qb, k_refs[r][...].astype(BF16), NT_DIMS, preferred_element_type=F32) + bias

    def page_values(r):
        return v_refs[r][...].astype(BF16)

    @pl.when(unstabilised)
    def _():
        pv = jnp.zeros((rows, DV_DIFF), F32)
        psum = jnp.zeros((rows, LANES), F32)
        for r in range(pages):
            p = jnp.exp2(page_scores(r))
            for c0 in range(0, cols, LANES):
                psum = psum + p[:, c0:c0 + LANES]
            pv = pv + jnp.dot(p.astype(BF16), page_values(r), preferred_element_type=F32)
        l_ref[...] += jnp.sum(psum, axis=-1, keepdims=True)
        acc_ref[...] += pv

    @pl.when(jnp.logical_not(unstabilised))
    def _():
        s = jnp.concatenate([page_scores(r) for r in range(pages)], axis=1)
        m_prev = m_ref[...]
        m_new = jnp.maximum(m_prev, jnp.max(s, axis=-1, keepdims=True))
        alpha = jnp.exp2(m_prev - m_new)
        p = jnp.exp2(s - m_new)
        l_ref[...] = alpha * l_ref[...] + jnp.sum(p, axis=-1, keepdims=True)
        pb = p.astype(BF16)
        pv = jnp.dot(pb[:, :cols], page_values(0), preferred_element_type=F32)
        for r in range(1, pages):
            pv = pv + jnp.dot(pb[:, r * cols:(r + 1) * cols], page_values(r), preferred_element_type=F32)
        acc_ref[...] = alpha * acc_ref[...] + pv
        m_ref[...] = m_new

    @pl.when(p_step == pl.num_programs(1) - 1)
    def _():
        on = acc_ref[...] / l_ref[...]
        lam = _lambda_value(lam_ref, lam_init)
        gain = gn_ref[...]
        for h in range(H_DIFF):
            o = on[h * hr:h * hr + ts] - lam * on[h * hr + ts:(h + 1) * hr]
            y_ref[:, h * DV_DIFF:(h + 1) * DV_DIFF] = _head_norm(o, gain, lam_init).astype(BF16)


def _diff_sample(z16, cache_k, cache_v, page_table, logit_bound, lam_vecs, gn, batch, ts, lam_init):
    n_pages = page_table.shape[1]
    page_rows = cache_k.shape[1]
    width = H_DIFF * DV_DIFF
    pages = PAGES_PER_STEP
    assert n_pages % pages == 0
    rows = 2 * H_DIFF * ts
    z3 = z16.reshape(batch, ts, N_SECTIONS * SECTION)
    new_spec = lambda sec: pl.BlockSpec((None, ts, SECTION), lambda b, p, pt: (b, 0, sec))

    def page_spec(r):
        return pl.BlockSpec((None, page_rows, DV_DIFF), lambda b, p, pt: (pt[b, p * pages + r], 0, 0))

    grid_spec = pltpu.PrefetchScalarGridSpec(
        num_scalar_prefetch=1,
        grid=(batch, n_pages // pages),
        in_specs=[
            pl.BlockSpec(memory_space=pltpu.SMEM),
            pl.BlockSpec((4, D_HEAD), lambda b, p, pt: (0, 0)),
            new_spec(3), new_spec(4), new_spec(5),
            pl.BlockSpec((1, DV_DIFF), lambda b, p, pt: (0, 0)),
        ] + [page_spec(r) for r in range(pages)] + [page_spec(r) for r in range(pages)],
        out_specs=pl.BlockSpec((None, ts, width), lambda b, p, pt: (b, 0, 0)),
        scratch_shapes=[
            pltpu.VMEM((rows, LANES), F32),
            pltpu.VMEM((rows, page_rows), F32),
            pltpu.VMEM((rows, 1), F32),
            pltpu.VMEM((rows, 1), F32),
            pltpu.VMEM((rows, DV_DIFF), F32),
        ],
    )
    y = pl.pallas_call(
        functools.partial(_diff_sample_kernel, ts=ts, lam_init=lam_init),
        grid_spec=grid_spec,
        out_shape=jax.ShapeDtypeStruct((batch, ts, width), BF16),
        compiler_params=_params(("parallel", "arbitrary")),
        name="diff_attention_sample",
    )(page_table, logit_bound, lam_vecs, z3, z3, z3, gn, *([cache_k] * pages), *([cache_v] * pages))
    return y.reshape(batch * ts, width)


def _prompt_schedule(batch, seq, tq, n_steps):
    nq = seq // tq
    budget = n_steps // (batch * H_DIFF)
    per_tile = {qi: ATTN_SLOTS for qi in range(nq)}
    spare = budget - sum(-(-(qi + 1) // ATTN_SLOTS) for qi in range(nq))
    if spare < 0:
        return None
    for qi in reversed(range(nq)):
        extra = -(-(qi + 1) // (ATTN_SLOTS - 1)) - -(-(qi + 1) // ATTN_SLOTS)
        if 0 < extra <= spare:
            per_tile[qi] = ATTN_SLOTS - 1
            spare -= extra
    rec = []
    for bp in range(batch):
        for h in range(H_DIFF):
            for qi in range(nq):
                nblk = qi + 1
                size = per_tile[qi]
                groups = -(-nblk // size)
                for g in range(groups):
                    kbs = [g * size + j for j in range(ATTN_SLOTS)]
                    used = min(size, nblk - g * size)
                    rec.append([bp * nq + qi, h, bp]
                               + [min(kb, nblk - 1) for kb in kbs]
                               + [(qi - kb) * tq if j < used else NEVER for j, kb in enumerate(kbs)]
                               + [int(g == 0), int(g == groups - 1), used])
    idle = rec[-1][:6] + [NEVER] * ATTN_SLOTS + [0, 0, 0]
    rec += [idle] * (n_steps - len(rec))
    return np.asarray(rec, np.int32).reshape(-1)


def _fused_attn_kernel(pt_ref, sched_ref, lam_ref, q_ref, kn_ref, vn_ref, gn_ref, pq_ref, pk_ref, pv_ref,
                       *rest, ts, lam_init):
    del pt_ref
    pages = PAGES_PER_STEP
    k_refs = rest[:pages]
    v_refs = rest[pages:2 * pages]
    y_ref, yp_ref, q_scr, bias_scr, l_ref, acc_ref, rel_scr, pacc_ref = rest[2 * pages:]
    p_step = pl.program_id(1)
    step = pl.program_id(0) * pl.num_programs(1) + p_step
    hr = 2 * ts
    rows = H_DIFF * hr
    tq = pq_ref.shape[0]

    def head_cols(x, h):
        return x[:, h * DV_DIFF:(h + 1) * DV_DIFF]

    @pl.when(step == 0)
    def _():
        rel_scr[...] = (lax.broadcasted_iota(jnp.int32, rel_scr.shape, 1)
                        - lax.broadcasted_iota(jnp.int32, rel_scr.shape, 0) % tq)
        pacc_ref[...] = jnp.zeros(pacc_ref.shape, F32)

    @pl.when(p_step == 0)
    def _():
        q32 = q_ref[...].astype(F32)
        r8 = lax.broadcasted_iota(jnp.int32, (hr, LANES), 0)
        lane8 = lax.broadcasted_iota(jnp.int32, (hr, LANES), 1)
        keep = (lane8 // D_HEAD) == (r8 // ts)
        q_heads = []
        for h in range(H_DIFF):
            blk = head_cols(q32, h)
            qt = jnp.zeros((hr, LANES), F32)
            for t in range(ts):
                qt = jnp.where(r8 % ts == t, blk[t:t + 1, :], qt)
            q_heads.append(jnp.where(keep, qt, 0.0))
        q_all = jnp.concatenate(q_heads, axis=0)
        q_scr[...] = q_all
        brow = lax.broadcasted_iota(jnp.int32, bias_scr.shape, 0)
        bcol = lax.broadcasted_iota(jnp.int32, bias_scr.shape, 1)
        bias_scr[...] = jnp.where(bcol % H_DIFF == brow // hr, 0.0, -jnp.inf)

        kn = kn_ref[...].astype(F32)
        vn = vn_ref[...].astype(F32)

        def per_row(x, t):
            return jnp.concatenate(
                [jnp.broadcast_to(head_cols(x, h)[t:t + 1, :], (hr, LANES)) for h in range(H_DIFF)], axis=0)

        qpos = lax.broadcasted_iota(jnp.int32, (rows, 1), 0) % ts
        l0 = jnp.zeros((rows, 1), F32)
        a0 = jnp.zeros((rows, LANES), F32)
        for t in range(ts):
            s_new = jnp.sum(q_all * per_row(kn, t), axis=-1, keepdims=True)
            pt = jnp.where(qpos >= t, jnp.exp2(s_new), 0.0)
            l0 = l0 + pt
            a0 = a0 + pt * per_row(vn, t)
        l_ref[...] = l0
        acc_ref[...] = a0

    qb = q_scr[...].astype(BF16)
    bias = bias_scr[...]
    cols = bias.shape[1]
    pv = jnp.zeros((rows, DV_DIFF), F32)
    psum = jnp.zeros((rows, LANES), F32)
    for r in range(pages):
        s = lax.dot_general(qb, k_refs[r][...].astype(BF16), NT_DIMS, preferred_element_type=F32) + bias
        p = jnp.exp2(s)
        for c0 in range(0, cols, LANES):
            psum = psum + p[:, c0:c0 + LANES]
        pv = pv + jnp.dot(p.astype(BF16), v_refs[r][...].astype(BF16), preferred_element_type=F32)
    l_ref[...] += jnp.sum(psum, axis=-1, keepdims=True)
    acc_ref[...] += pv

    rec = step * SCHED_REC
    used = sched_ref[rec + 5 + 2 * ATTN_SLOTS]

    def prompt_blocks(n):
        q = pq_ref[...]
        lane = lax.broadcasted_iota(jnp.int32, q.shape, 1)
        zero = jnp.zeros_like(q)
        qs = jnp.concatenate([jnp.where(lane < D_HEAD, q, zero), jnp.where(lane >= D_HEAD, q, zero)], axis=0)
        ones = jnp.ones((tq, LANES), BF16)
        rel = rel_scr[...]
        pacc = jnp.where(sched_ref[rec + 3 + 2 * ATTN_SLOTS] == 1, 0.0, pacc_ref[...])
        for j in range(n):
            off = pl.multiple_of(sched_ref[rec + 3 + j] * tq, tq)
            visible = rel <= sched_ref[rec + 3 + ATTN_SLOTS + j]
            s = lax.dot_general(qs, pk_ref[pl.ds(off, tq), :], NT_DIMS, preferred_element_type=F32)
            p = jnp.exp2(jnp.where(visible, s, -jnp.inf)).astype(BF16)
            pacc = pacc + jnp.dot(p, jnp.concatenate([pv_ref[pl.ds(off, tq), :], ones], axis=1),
                                  preferred_element_type=F32)
        pacc_ref[...] = pacc

    for n in range(1, ATTN_SLOTS + 1):
        pl.when(used == n)(functools.partial(prompt_blocks, n))

    @pl.when(p_step == pl.num_programs(1) - 1)
    def _():
        on = acc_ref[...] / l_ref[...]
        lam = _lambda_value(lam_ref, lam_init)
        gain = gn_ref[...]
        for h in range(H_DIFF):
            o = on[h * hr:h * hr + ts] - lam * on[h * hr + ts:(h + 1) * hr]
            y_ref[:, h * DV_DIFF:(h + 1) * DV_DIFF] = _head_norm(o, gain, lam_init).astype(BF16)

    @pl.when(sched_ref[rec + 4 + 2 * ATTN_SLOTS] == 1)
    def _():
        acc = pacc_ref[...]
        on = acc[:, :DV_DIFF] / acc[:, DV_DIFF:]
        lam = _lambda_value(lam_ref, lam_init)
        o = on[:tq] - lam * on[tq:]
        yp_ref[...] = _head_norm(o, gn_ref[...], lam_init).astype(BF16)


def _fused_attention(z16_s, z16_p, cache_k, cache_v, page_table, sched, lam_vecs, gn, batch_s, ts, batch_p, seq,
                     lam_init):
    n_pages = page_table.shape[1]
    page_rows = cache_k.shape[1]
    width = H_DIFF * DV_DIFF
    pages = PAGES_PER_STEP
    n_steps = n_pages // pages
    rows = 2 * H_DIFF * ts
    tq = min(ATTN_TILE, seq)
    q_col = 3 * SECTION // LANES
    k_col = 4 * SECTION // LANES
    v_col = 5 * SECTION // LANES
    z3 = z16_s.reshape(batch_s, ts, N_SECTIONS * SECTION)
    new_spec = lambda sec: pl.BlockSpec((None, ts, SECTION), lambda b, p, pt, sc: (b, 0, sec))

    def field(b, p, sc, i):
        return sc[(b * n_steps + p) * SCHED_REC + i]

    def page_spec(r):
        return pl.BlockSpec((None, page_rows, DV_DIFF), lambda b, p, pt, sc: (pt[b, p * pages + r], 0, 0))

    grid_spec = pltpu.PrefetchScalarGridSpec(
        num_scalar_prefetch=2,
        grid=(batch_s, n_steps),
        in_specs=[
            pl.BlockSpec((4, D_HEAD), lambda b, p, pt, sc: (0, 0)),
            new_spec(3), new_spec(4), new_spec(5),
            pl.BlockSpec((1, DV_DIFF), lambda b, p, pt, sc: (0, 0)),
            pl.BlockSpec((tq, LANES), lambda b, p, pt, sc: (field(b, p, sc, 0), q_col + field(b, p, sc, 1))),
            pl.BlockSpec((seq, LANES), lambda b, p, pt, sc: (field(b, p, sc, 2), k_col + field(b, p, sc, 1))),
            pl.BlockSpec((seq, LANES), lambda b, p, pt, sc: (field(b, p, sc, 2), v_col + field(b, p, sc, 1))),
        ] + [page_spec(r) for r in range(pages)] + [page_spec(r) for r in range(pages)],
        out_specs=[
            pl.BlockSpec((None, ts, width), lambda b, p, pt, sc: (b, 0, 0)),
            pl.BlockSpec((tq, DV_DIFF), lambda b, p, pt, sc: (field(b, p, sc, 0), field(b, p, sc, 1))),
        ],
        scratch_shapes=[
            pltpu.VMEM((rows, LANES), F32),
            pltpu.VMEM((rows, page_rows), F32),
            pltpu.VMEM((rows, 1), F32),
            pltpu.VMEM((rows, DV_DIFF), F32),
            pltpu.VMEM((2 * tq, tq), jnp.int32),
            pltpu.VMEM((2 * tq, 2 * DV_DIFF), F32),
        ],
    )
    y_s, y_p = pl.pallas_call(
        functools.partial(_fused_attn_kernel, ts=ts, lam_init=lam_init),
        grid_spec=grid_spec,
        out_shape=[jax.ShapeDtypeStruct((batch_s, ts, width), BF16),
                   jax.ShapeDtypeStruct((batch_p * seq, width), BF16)],
        compiler_params=_params(("arbitrary", "arbitrary")),
        name="fused_attention",
    )(page_table, sched, lam_vecs, z3, z3, z3, gn, z16_p, z16_p, z16_p,
      *([cache_k] * pages), *([cache_v] * pages))
    return y_s.reshape(batch_s * ts, width), y_p


def _merge_kernel(x_ref, yr_ref, yd_ref, gr_ref, gd_ref, wr_ref, wd_ref, wo_ref, g_ref, x1_ref, h_ref):
    a = jnp.dot(yr_ref[...], wr_ref[...], preferred_element_type=F32)
    b = jnp.dot(yd_ref[...], wd_ref[...], preferred_element_type=F32)
    m = gr_ref[...].astype(F32) * a + gd_ref[...].astype(F32) * b
    x1 = x_ref[...] + jnp.dot(m.astype(BF16), wo_ref[...], preferred_element_type=F32)
    x1_ref[...] = x1
    ms = jnp.mean(x1 * x1, axis=-1, keepdims=True)
    h_ref[...] = (x1 * lax.rsqrt(ms + EPS) * g_ref[...]).astype(BF16)


def _merge(x, y_r, y_d, z16, w_br_ret, w_br_diff, w_out, g_ffn):
    n = x.shape[0]
    tm = min(TOKEN_TILE, n)
    tile = lambda col: pl.BlockSpec((tm, D_MODEL), lambda i: (i, col))
    weight = pl.BlockSpec((D_MODEL, D_MODEL), lambda i: (0, 0))
    return pl.pallas_call(
        _merge_kernel,
        grid=(n // tm,),
        in_specs=[tile(0), tile(0), tile(0), tile(6), tile(7), weight, weight, weight,
                  pl.BlockSpec((1, D_MODEL), lambda i: (0, 0))],
        out_specs=[tile(0), tile(0)],
        out_shape=[jax.ShapeDtypeStruct((n, D_MODEL), F32), jax.ShapeDtypeStruct((n, D_MODEL), BF16)],
        compiler_params=_params(("parallel",)),
        name="merge",
    )(x, y_r, y_d, z16, z16, w_br_ret, w_br_diff, w_out, g_ffn)


def _ffn_chunks(d_ff):
    edges = list(range(0, d_ff, SECTION)) + [d_ff]
    return list(zip(edges[:-1], edges[1:]))


def _ffn_kernel(x_ref, h_ref, wg_ref, wu_ref, wd_ref, o_ref):
    h = h_ref[...]
    acc = x_ref[...]
    for c0, c1 in _ffn_chunks(wg_ref.shape[1]):
        g = jnp.dot(h, wg_ref[:, c0:c1], preferred_element_type=F32)
        u = jnp.dot(h, wu_ref[:, c0:c1], preferred_element_type=F32)
        a = (g * _sigmoid(g) * u).astype(BF16)
        acc = acc + jnp.dot(a, wd_ref[c0:c1, :], preferred_element_type=F32)
    o_ref[...] = acc


def _ffn(x1, h, wg, wu, wd):
    n = x1.shape[0]
    d_ff = wg.shape[1]
    tm = min(TOKEN_TILE, n)
    tile = pl.BlockSpec((tm, D_MODEL), lambda i: (i, 0))
    single = pl.Buffered(1)
    return pl.pallas_call(
        _ffn_kernel,
        grid=(n // tm,),
        in_specs=[tile, tile,
                  pl.BlockSpec((D_MODEL, d_ff), lambda i: (0, 0), pipeline_mode=single),
                  pl.BlockSpec((D_MODEL, d_ff), lambda i: (0, 0), pipeline_mode=single),
                  pl.BlockSpec((d_ff, D_MODEL), lambda i: (0, 0), pipeline_mode=single)],
        out_specs=tile,
        out_shape=jax.ShapeDtypeStruct((n, D_MODEL), F32),
        compiler_params=_params(("parallel",)),
        name="ffn",
    )(x1, h, wg, wu, wd)


def _rope_tables(pos, half):
    inv = 1.0 / jnp.power(ROPE_THETA, jnp.arange(half, dtype=F32) / half)
    ang = pos.astype(F32)[:, None] * inv[None, :]
    cos = jnp.cos(ang)
    sin = jnp.sin(ang)
    reps = LANES // (2 * half)
    return (jnp.tile(jnp.concatenate([cos, cos], axis=-1), (1, reps)),
            jnp.tile(jnp.concatenate([-sin, sin], axis=-1), (1, reps)))


def _layer_tables(pos):
    cr, sr = _rope_tables(pos, DK_RET // 2)
    cd, sd = _rope_tables(pos, D_HEAD // 2)
    return (cr, sr, cd, sd)


def kernel(x_prompt, x_sample, cache_k, cache_v, state_ret, page_table, norm_mix_g, w_in, ret_gn_g, diff_qnorm_g, diff_knorm_g, lambda_q1, lambda_k1, lambda_q2, lambda_k2, diff_gn_g, w_br_ret, w_br_diff, w_out, norm_ffn_g, w_ffn_gate, w_ffn_up, w_ffn_down):
    B, T, D = x_prompt.shape
    Bs, Ts, _ = x_sample.shape
    depth, n_phys, page = cache_k.shape[:3]
    past_len = page_table.shape[1] * page

    tabs_p = _layer_tables(jnp.arange(T, dtype=jnp.int32))
    tabs_s = _layer_tables(jnp.tile(past_len + jnp.arange(Ts, dtype=jnp.int32), Bs))
    lg = jnp.log(1.0 - jnp.power(2.0, -5.0 - jnp.arange(H_RET, dtype=F32)))
    lg_tab = jnp.broadcast_to(lg[:, None, None], (H_RET, 1, LANES))
    gidx = jnp.arange(NORM_GROUP_TILE) // D_HEAD
    bd = (gidx[:, None] == gidx[None, :]).astype(BF16)
    ck = cache_k.reshape(depth * n_phys, page * H_DIFF, DV_DIFF)
    cv = cache_v.reshape(depth * n_phys, page * H_DIFF, DV_DIFF)

    sched = None
    if min(ATTN_TILE, T) == min(ATTN_KV_TILE, T) and page_table.shape[1] % PAGES_PER_STEP == 0:
        sched = _prompt_schedule(B, T, min(ATTN_TILE, T), Bs * (page_table.shape[1] // PAGES_PER_STEP))
    xp = x_prompt.reshape(B * T, D)
    xs = x_sample.reshape(Bs * Ts, D)
    outs = [[] for _ in range(6)]
    for l in range(depth):
        lam_init = 0.8 - 0.6 * math.exp(-0.3 * l)
        w_in_l = w_in[l].astype(BF16)
        g_mix = norm_mix_g[l].reshape(1, D)
        g_ffn = norm_ffn_g[l].reshape(1, D)
        qg = jnp.tile(diff_qnorm_g[l], LANES // D_HEAD).reshape(1, LANES)
        kg = jnp.tile(diff_knorm_g[l], LANES // D_HEAD).reshape(1, LANES)
        gn_r = ret_gn_g[l].reshape(H_RET, 1, DV_RET)
        gn_d = diff_gn_g[l].reshape(1, DV_DIFF)
        logit_bound = (1.02 * LOG2E * D_HEAD ** 0.5 * jnp.max(jnp.abs(diff_qnorm_g[l]))
                       * jnp.max(jnp.abs(diff_knorm_g[l]))).astype(F32).reshape(1)
        lam_vecs = jnp.stack([lambda_q1[l], lambda_k1[l], lambda_q2[l], lambda_k2[l]]).astype(F32)
        wr = w_br_ret[l].astype(BF16)
        wdf = w_br_diff[l].astype(BF16)
        wo = w_out[l].astype(BF16)
        wg = w_ffn_gate[l].astype(BF16)
        wu = w_ffn_up[l].astype(BF16)
        wd = w_ffn_down[l].astype(BF16)

        zp, k32_p, v32_p = _in_projection(xp, g_mix, w_in_l, tabs_p, qg, kg, bd, T)
        zs, k32_s, v32_s = _in_projection(xs, g_mix, w_in_l, tabs_s, qg, kg, bd, Bs * Ts)
        yr_p, s_p = _retention_prompt(zp, lg_tab, gn_r, jnp.zeros((B, H_RET, DK_RET, DV_RET), F32), B, T)
        yr_s, s_s = _retention_sample(zs, lg_tab, gn_r, state_ret[l], Bs, Ts)
        pt_l = page_table + l * n_phys

        def separate_attention():
            return (_diff_sample(zs, ck, cv, pt_l, logit_bound, lam_vecs, gn_d, Bs, Ts, lam_init),
                    _diff_prompt(zp, logit_bound, lam_vecs, gn_d, B, T, lam_init))

        def fused_attention():
            return _fused_attention(zs, zp, ck, cv, pt_l, jnp.asarray(sched), lam_vecs, gn_d, Bs, Ts, B, T, lam_init)

        if sched is None:
            yd_s, yd_p = separate_attention()
        else:
            yd_s, yd_p = lax.cond(logit_bound[0] <= SAFE_LOGIT_BOUND, fused_attention, separate_attention)

        x1, h2 = _merge(xp, yr_p, yd_p, zp, wr, wdf, wo, g_ffn)
        xp = _ffn(x1, h2, wg, wu, wd)
        x1, h2 = _merge(xs, yr_s, yd_s, zs, wr, wdf, wo, g_ffn)
        xs = _ffn(x1, h2, wg, wu, wd)
        outs[0].append(k32_p.reshape(B, T, H_DIFF, 2 * D_HEAD))
        outs[1].append(v32_p.reshape(B, T, H_DIFF, DV_DIFF))
        outs[2].append(s_p)
        outs[3].append(k32_s.reshape(Bs, Ts, H_DIFF, 2 * D_HEAD))
        outs[4].append(v32_s.reshape(Bs, Ts, H_DIFF, DV_DIFF))
        outs[5].append(s_s)

    stacked = [jnp.stack(o) for o in outs]
    return (xp.reshape(B, T, D), xs.reshape(Bs, Ts, D),
            stacked[0], stacked[1], stacked[2], stacked[3], stacked[4], stacked[5])
```

```python
import functools
import math

import jax
import jax.numpy as jnp
import numpy as np
from jax import lax
from jax.experimental import pallas as pl
from jax.experimental.pallas import tpu as pltpu

F32 = jnp.float32
BF16 = jnp.bfloat16

D_MODEL = 1024
H_RET = 4
DK_RET = 128
DV_RET = 256
H_DIFF = 8
D_HEAD = 64
DV_DIFF = 2 * D_HEAD
ROPE_THETA = 10000.0
EPS = 1e-6
LOG2E = 1.4426950408889634
SAFE_LOGIT_BOUND = 64.0

LANES = 128
SECTION = 1024
N_SECTIONS = 8
NORM_GROUP_TILE = 256
VMEM_LIMIT = 56 * 1024 * 1024

TOKEN_TILE = 512
RET_CHUNK = 512
ATTN_TILE = 512
ATTN_KV_TILE = 512
PAGES_PER_STEP = 8
RET_SAMPLE_GROUP = 8
ATTN_SLOTS = 3
SCHED_REC = 6 + 2 * ATTN_SLOTS
NEVER = -(1 << 30)

NT_DIMS = (((1,), (1,)), ((), ()))
TN_DIMS = (((0,), (0,)), ((), ()))


def _sigmoid(x):
    return 1.0 / (1.0 + jnp.exp(-x))


def _params(semantics):
    return pltpu.CompilerParams(dimension_semantics=semantics, vmem_limit_bytes=VMEM_LIMIT)


def _inproj_kernel(x_ref, g_ref, w_ref, cr_ref, sr_ref, cd_ref, sd_ref, qg_ref, kg_ref, bd_ref,
                   z_ref, k32_ref, v32_ref):
    x = x_ref[...]
    ms = jnp.mean(x * x, axis=-1, keepdims=True)
    h = (x * lax.rsqrt(ms + EPS) * g_ref[...]).astype(BF16)

    def project(j):
        return jnp.dot(h, w_ref[:, j * SECTION:(j + 1) * SECTION], preferred_element_type=F32)

    def put(j, col, width, val):
        z_ref[:, j * SECTION + col:j * SECTION + col + width] = val.astype(BF16)

    acc = project(0)
    c = cr_ref[...]
    s = sr_ref[...]
    for hh in range(SECTION // LANES):
        blk = acc[:, hh * LANES:(hh + 1) * LANES]
        r = blk * c + pltpu.roll(blk, DK_RET // 2, 1) * s
        if hh >= H_RET:
            r = r * (DK_RET ** -0.5)
        put(0, hh * LANES, LANES, r)

    put(1, 0, SECTION, project(1))
    acc = project(2)
    put(2, 0, SECTION, acc * _sigmoid(acc))

    def qk_norm_rope(acc, gain, scale, emit):
        c = cd_ref[...]
        s = sd_ref[...]
        lane = lax.broadcasted_iota(jnp.int32, c.shape, 1)
        first_half = (lane % D_HEAD) < (D_HEAD // 2)
        for t in range(SECTION // NORM_GROUP_TILE):
            blk = acc[:, t * NORM_GROUP_TILE:(t + 1) * NORM_GROUP_TILE]
            ms = jnp.dot((blk * blk).astype(BF16), bd_ref[...], preferred_element_type=F32) * (1.0 / D_HEAD)
            y = blk * lax.rsqrt(ms + EPS)
            for u in range(NORM_GROUP_TILE // LANES):
                yb = y[:, u * LANES:(u + 1) * LANES] * gain
                partner = jnp.where(first_half,
                                    pltpu.roll(yb, LANES - D_HEAD // 2, 1),
                                    pltpu.roll(yb, D_HEAD // 2, 1))
                r = yb * c + partner * s
                if scale != 1.0:
                    r = r * scale
                emit(t * NORM_GROUP_TILE + u * LANES, r)

    qk_norm_rope(project(3), qg_ref[...], D_HEAD ** -0.5 * LOG2E, lambda col, r: put(3, col, LANES, r))

    def emit_k(col, r):
        k32_ref[:, col // LANES, :] = r
        put(4, col, LANES, r)
    qk_norm_rope(project(4), kg_ref[...], 1.0, emit_k)

    acc = project(5)
    for hh in range(H_DIFF):
        v32_ref[:, hh, :] = acc[:, hh * DV_DIFF:(hh + 1) * DV_DIFF]
    put(5, 0, SECTION, acc)

    for j in (6, 7):
        put(j, 0, SECTION, _sigmoid(project(j)))


def _in_projection(x, g, w_bf16, tabs, qg, kg, bd, table_rows):
    n = x.shape[0]
    tm = min(TOKEN_TILE, n)
    n_tab_blocks = table_rows // tm
    tab_spec = pl.BlockSpec((tm, LANES), lambda i: (i % n_tab_blocks, 0))
    const = lambda shape: pl.BlockSpec(shape, lambda i: (0, 0))
    return pl.pallas_call(
        _inproj_kernel,
        grid=(n // tm,),
        in_specs=[
            pl.BlockSpec((tm, D_MODEL), lambda i: (i, 0)),
            const((1, D_MODEL)),
            pl.BlockSpec((D_MODEL, N_SECTIONS * SECTION), lambda i: (0, 0), pipeline_mode=pl.Buffered(1)),
            tab_spec, tab_spec, tab_spec, tab_spec,
            const((1, LANES)), const((1, LANES)),
            const((NORM_GROUP_TILE, NORM_GROUP_TILE)),
        ],
        out_specs=[
            pl.BlockSpec((tm, N_SECTIONS * SECTION), lambda i: (i, 0)),
            pl.BlockSpec((tm, H_DIFF, DV_DIFF), lambda i: (i, 0, 0)),
            pl.BlockSpec((tm, H_DIFF, DV_DIFF), lambda i: (i, 0, 0)),
        ],
        out_shape=[
            jax.ShapeDtypeStruct((n, N_SECTIONS * SECTION), BF16),
            jax.ShapeDtypeStruct((n, H_DIFF, DV_DIFF), F32),
            jax.ShapeDtypeStruct((n, H_DIFF, DV_DIFF), F32),
        ],
        compiler_params=_params(("parallel",)),
        name="in_projection",
    )(x, g, w_bf16, *tabs, qg, kg, bd)


def _retention_kernel(lg_ref, q_ref, k_ref, v_ref, sg_ref, gn_ref, s0_ref, y_ref, s_ref, decay_ref):
    c = pl.program_id(2)
    chunk = q_ref.shape[0]
    lg = lg_ref[...][:, :1]

    @pl.when(c == 0)
    def _():
        s_ref[...] = s0_ref[...]
        row = lax.broadcasted_iota(jnp.int32, (chunk, chunk), 0)
        col = lax.broadcasted_iota(jnp.int32, (chunk, chunk), 1)
        dist = (row - col).astype(F32)
        decay_ref[...] = jnp.where(dist >= 0, jnp.exp(jnp.maximum(dist, 0.0) * lg), 0.0)

    idx = lax.broadcasted_iota(jnp.int32, (chunk, 1), 0).astype(F32)
    xi = jnp.exp((idx + 1.0) * lg)
    zeta = jnp.exp((chunk - 1.0 - idx) * lg)
    chunk_decay = jnp.exp(chunk * lg)

    q = q_ref[...]
    k = k_ref[...]
    v = v_ref[...]
    state = s_ref[...]
    att = lax.dot_general(q, k, NT_DIMS, preferred_element_type=F32) * decay_ref[...]
    o = (jnp.dot(att.astype(BF16), v, preferred_element_type=F32)
         + jnp.dot(q, state.astype(BF16), preferred_element_type=F32) * xi)
    kz = (k.astype(F32) * zeta).astype(BF16)
    s_ref[...] = chunk_decay * state + lax.dot_general(kz, v, TN_DIMS, preferred_element_type=F32)

    ms = jnp.mean(o * o, axis=-1, keepdims=True)
    y = o * lax.rsqrt(ms + EPS) * gn_ref[...] * sg_ref[...].astype(F32)
    y_ref[...] = y.astype(BF16)


def _retention_prompt(z16, lg_tab, gn, s0, batch, seq):
    chunk = min(RET_CHUNK, seq)
    n_chunks = seq // chunk
    row = lambda b, h, c: b * n_chunks + c
    return pl.pallas_call(
        _retention_kernel,
        grid=(batch, H_RET, n_chunks),
        in_specs=[
            pl.BlockSpec((None, 1, LANES), lambda b, h, c: (h, 0, 0)),
            pl.BlockSpec((chunk, DK_RET), lambda b, h, c: (row(b, h, c), h)),
            pl.BlockSpec((chunk, DK_RET), lambda b, h, c: (row(b, h, c), H_RET + h)),
            pl.BlockSpec((chunk, DV_RET), lambda b, h, c: (row(b, h, c), SECTION // DV_RET + h)),
            pl.BlockSpec((chunk, DV_RET), lambda b, h, c: (row(b, h, c), 2 * SECTION // DV_RET + h)),
            pl.BlockSpec((None, 1, DV_RET), lambda b, h, c: (h, 0, 0)),
            pl.BlockSpec((None, None, DK_RET, DV_RET), lambda b, h, c: (b, h, 0, 0)),
        ],
        out_specs=[
            pl.BlockSpec((chunk, DV_RET), lambda b, h, c: (row(b, h, c), h)),
            pl.BlockSpec((None, None, DK_RET, DV_RET), lambda b, h, c: (b, h, 0, 0)),
        ],
        out_shape=[
            jax.ShapeDtypeStruct((batch * seq, H_RET * DV_RET), BF16),
            jax.ShapeDtypeStruct((batch, H_RET, DK_RET, DV_RET), F32),
        ],
        scratch_shapes=[pltpu.VMEM((chunk, chunk), F32)],
        compiler_params=_params(("parallel", "parallel", "arbitrary")),
        name="retention_prompt",
    )(lg_tab, z16, z16, z16, z16, gn, s0)


def _retention_sample_kernel(lg_ref, q_ref, k_ref, v_ref, sg_ref, gn_ref, s0_ref, y_ref, s_ref, *, ts):
    rows = q_ref.shape[0]
    group = rows // ts
    row = lax.broadcasted_iota(jnp.int32, (rows, rows), 0)
    col = lax.broadcasted_iota(jnp.int32, (rows, rows), 1)
    same_seq = (row // ts) == (col // ts)
    dist = ((row % ts) - (col % ts)).astype(F32)
    ridx = lax.broadcasted_iota(jnp.int32, (rows, 1), 0)
    tpos = (ridx % ts).astype(F32)
    seq_of_row = ridx // ts

    for h in range(H_RET):
        lg = lg_ref[h][:, :1]
        decay = jnp.where(same_seq & (dist >= 0), jnp.exp(jnp.maximum(dist, 0.0) * lg), 0.0)
        xi = jnp.exp((tpos + 1.0) * lg)
        zeta = jnp.exp((ts - 1.0 - tpos) * lg)
        seq_decay = jnp.exp(ts * lg)
        q = q_ref[:, h * DK_RET:(h + 1) * DK_RET]
        k = k_ref[:, h * DK_RET:(h + 1) * DK_RET]
        v = v_ref[:, h * DV_RET:(h + 1) * DV_RET]
        att = lax.dot_general(q, k, NT_DIMS, preferred_element_type=F32) * decay
        o = jnp.dot(att.astype(BF16), v, preferred_element_type=F32)
        kz = k.astype(F32) * zeta
        for bb in range(group):
            mine = seq_of_row == bb
            state = s0_ref[bb, h]
            full = jnp.dot(q, state.astype(BF16), preferred_element_type=F32) * xi
            o = o + jnp.where(mine, full, 0.0)
            kz_b = jnp.where(mine, kz, 0.0).astype(BF16)
            s_ref[bb, h] = seq_decay * state + lax.dot_general(kz_b, v, TN_DIMS, preferred_element_type=F32)
        ms = jnp.mean(o * o, axis=-1, keepdims=True)
        y = o * lax.rsqrt(ms + EPS) * gn_ref[h] * sg_ref[:, h * DV_RET:(h + 1) * DV_RET].astype(F32)
        y_ref[:, h * DV_RET:(h + 1) * DV_RET] = y.astype(BF16)


def _retention_sample(z16, lg_tab, gn, s0, batch, ts):
    group = min(RET_SAMPLE_GROUP, batch)
    rows = group * ts
    return pl.pallas_call(
        functools.partial(_retention_sample_kernel, ts=ts),
        grid=(batch // group,),
        in_specs=[
            pl.BlockSpec((H_RET, 1, LANES), lambda i: (0, 0, 0)),
            pl.BlockSpec((rows, H_RET * DK_RET), lambda i: (i, 0)),
            pl.BlockSpec((rows, H_RET * DK_RET), lambda i: (i, 1)),
            pl.BlockSpec((rows, H_RET * DV_RET), lambda i: (i, 1)),
            pl.BlockSpec((rows, H_RET * DV_RET), lambda i: (i, 2)),
            pl.BlockSpec((H_RET, 1, DV_RET), lambda i: (0, 0, 0)),
            pl.BlockSpec((group, H_RET, DK_RET, DV_RET), lambda i: (i, 0, 0, 0)),
        ],
        out_specs=[
            pl.BlockSpec((rows, H_RET * DV_RET), lambda i: (i, 0)),
            pl.BlockSpec((group, H_RET, DK_RET, DV_RET), lambda i: (i, 0, 0, 0)),
        ],
        out_shape=[
            jax.ShapeDtypeStruct((batch * ts, H_RET * DV_RET), BF16),
            jax.ShapeDtypeStruct((batch, H_RET, DK_RET, DV_RET), F32),
        ],
        compiler_params=_params(("parallel",)),
        name="retention_sample",
    )(lg_tab, z16, z16, z16, z16, gn, s0)


def _lambda_value(lam_ref, lam_init):
    lv = lam_ref[...]
    a = jnp.sum(lv[0:1] * lv[1:2], axis=-1, keepdims=True)
    b = jnp.sum(lv[2:3] * lv[3:4], axis=-1, keepdims=True)
    return jnp.exp(a) - jnp.exp(b) + lam_init


def _head_norm(o, gain, lam_init):
    ms = jnp.mean(o * o, axis=-1, keepdims=True)
    return o * lax.rsqrt(ms + EPS) * gain * (1.0 - lam_init)


def _diff_prompt_kernel(bound_ref, lam_ref, q_ref, k_ref, v_ref, gn_ref, y_ref, acc_ref, *, lam_init, tk):
    qi = pl.program_id(2)
    tq = q_ref.shape[0]
    rows = 2 * tq
    q = q_ref[...]
    lane = lax.broadcasted_iota(jnp.int32, q.shape, 1)
    zero = jnp.zeros_like(q)
    qs = jnp.concatenate([jnp.where(lane < D_HEAD, q, zero), jnp.where(lane >= D_HEAD, q, zero)], axis=0)
    ones = jnp.ones((tk, LANES), BF16)
    acc_ref[...] = jnp.zeros(acc_ref.shape, F32)

    def lanes(x, n):
        return jnp.concatenate([x] * (n // LANES), axis=1)

    def scores(kb):
        k = k_ref[pl.ds(pl.multiple_of(kb * tk, tk), tk), :]
        return lax.dot_general(qs, k, NT_DIMS, preferred_element_type=F32)

    def values(kb):
        v = v_ref[pl.ds(pl.multiple_of(kb * tk, tk), tk), :]
        return jnp.concatenate([v, ones], axis=1)

    def causal(s):
        qpos = qi * tq + lax.broadcasted_iota(jnp.int32, s.shape, 0) % tq
        kpos = n_full * tk + lax.broadcasted_iota(jnp.int32, s.shape, 1)
        return jnp.where(kpos <= qpos, s, -jnp.inf)

    n_full = (qi * tq) // tk

    unstabilised = bound_ref[0] <= SAFE_LOGIT_BOUND

    def weights(kb, masked):
        s = scores(kb)
        return jnp.exp2(causal(s) if masked else s).astype(BF16)

    def accumulate(kb, p):
        acc_ref[...] += jnp.dot(p, values(kb), preferred_element_type=F32)

    @pl.when(unstabilised & (n_full == 0))
    def _():
        accumulate(0, weights(0, True))

    @pl.when(unstabilised & (n_full > 0))
    def _():
        def body(kb, p):
            p_next = weights(kb + 1, False)
            accumulate(kb, p)
            return p_next

        p = lax.fori_loop(0, n_full - 1, body, weights(0, False))
        p_last = weights(n_full, True)
        accumulate(n_full - 1, p)
        accumulate(n_full, p_last)

    @pl.when(bound_ref[0] > SAFE_LOGIT_BOUND)
    def _():
        def update(kb, s, m_prev):
            m_blk = jnp.max(s, axis=-1, keepdims=True)
            m_new = jnp.maximum(m_prev, jnp.broadcast_to(m_blk, (rows, LANES)))
            alpha = jnp.exp2(m_prev - m_new)
            p = jnp.exp2(s - lanes(m_new, tk)).astype(BF16)
            pv = jnp.dot(p, values(kb), preferred_element_type=F32)
            acc_ref[...] = lanes(alpha, 2 * LANES) * acc_ref[...] + pv
            return m_new

        m = lax.fori_loop(0, n_full, lambda kb, m_prev: update(kb, scores(kb), m_prev),
                          jnp.full((rows, LANES), -jnp.inf, F32))
        update(n_full, causal(scores(n_full)), m)

    acc = acc_ref[...]
    on = acc[:, :DV_DIFF] / acc[:, DV_DIFF:]
    lam = _lambda_value(lam_ref, lam_init)
    o = on[:tq] - lam * on[tq:]
    y_ref[...] = _head_norm(o, gn_ref[...], lam_init).astype(BF16)


def _diff_prompt(z16, logit_bound, lam_vecs, gn, batch, seq, lam_init):
    tq = min(ATTN_TILE, seq)
    tk = min(ATTN_KV_TILE, seq)
    nq = seq // tq
    q_col = 3 * SECTION // LANES
    k_col = 4 * SECTION // LANES
    v_col = 5 * SECTION // LANES
    return pl.pallas_call(
        functools.partial(_diff_prompt_kernel, lam_init=lam_init, tk=tk),
        grid=(batch, H_DIFF, nq),
        in_specs=[
            pl.BlockSpec(memory_space=pltpu.SMEM),
            pl.BlockSpec((4, D_HEAD), lambda b, h, i: (0, 0)),
            pl.BlockSpec((tq, LANES), lambda b, h, i: (b * nq + i, q_col + h)),
            pl.BlockSpec((seq, LANES), lambda b, h, i: (b, k_col + h)),
            pl.BlockSpec((seq, LANES), lambda b, h, i: (b, v_col + h)),
            pl.BlockSpec((1, DV_DIFF), lambda b, h, i: (0, 0)),
        ],
        out_specs=pl.BlockSpec((tq, DV_DIFF), lambda b, h, i: (b * nq + i, h)),
        out_shape=jax.ShapeDtypeStruct((batch * seq, H_DIFF * DV_DIFF), BF16),
        scratch_shapes=[pltpu.VMEM((2 * tq, 2 * DV_DIFF), F32)],
        compiler_params=_params(("parallel", "parallel", "arbitrary")),
        name="diff_attention_prompt",
    )(logit_bound, lam_vecs, z16, z16, z16, gn)


def _diff_sample_kernel(pt_ref, bound_ref, lam_ref, q_ref, kn_ref, vn_ref, gn_ref, *rest, ts, lam_init):
    del pt_ref
    pages = PAGES_PER_STEP
    k_refs = rest[:pages]
    v_refs = rest[pages:2 * pages]
    y_ref, q_scr, bias_scr, m_ref, l_ref, acc_ref = rest[2 * pages:]
    p_step = pl.program_id(1)
    hr = 2 * ts
    rows = H_DIFF * hr
    unstabilised = bound_ref[0] <= SAFE_LOGIT_BOUND

    def head_cols(x, h):
        return x[:, h * DV_DIFF:(h + 1) * DV_DIFF]

    @pl.when(p_step == 0)
    def _():
        q32 = q_ref[...].astype(F32)
        r8 = lax.broadcasted_iota(jnp.int32, (hr, LANES), 0)
        lane8 = lax.broadcasted_iota(jnp.int32, (hr, LANES), 1)
        keep = (lane8 // D_HEAD) == (r8 // ts)
        q_heads = []
        for h in range(H_DIFF):
            blk = head_cols(q32, h)
            qt = jnp.zeros((hr, LANES), F32)
            for t in range(ts):
                qt = jnp.where(r8 % ts == t, blk[t:t + 1, :], qt)
            q_heads.append(jnp.where(keep, qt, 0.0))
        q_all = jnp.concatenate(q_heads, axis=0)
        q_scr[...] = q_all
        brow = lax.broadcasted_iota(jnp.int32, bias_scr.shape, 0)
        bcol = lax.broadcasted_iota(jnp.int32, bias_scr.shape, 1)
        bias_scr[...] = jnp.where(bcol % H_DIFF == brow // hr, 0.0, -jnp.inf)

        kn = kn_ref[...].astype(F32)
        vn = vn_ref[...].astype(F32)

        def per_row(x, t):
            return jnp.concatenate(
                [jnp.broadcast_to(head_cols(x, h)[t:t + 1, :], (hr, LANES)) for h in range(H_DIFF)], axis=0)

        qpos = lax.broadcasted_iota(jnp.int32, (rows, 1), 0) % ts
        s_new = [jnp.sum(q_all * per_row(kn, t), axis=-1, keepdims=True) for t in range(ts)]
        m0 = s_new[0]
        for t in range(1, ts):
            m0 = jnp.where(qpos >= t, jnp.maximum(m0, s_new[t]), m0)
        m0 = jnp.where(unstabilised, 0.0, m0)
        l0 = jnp.zeros((rows, 1), F32)
        a0 = jnp.zeros((rows, LANES), F32)
        for t in range(ts):
            pt = jnp.where(qpos >= t, jnp.exp2(s_new[t] - m0), 0.0)
            l0 = l0 + pt
            a0 = a0 + pt * per_row(vn, t)
        m_ref[...] = m0
        l_ref[...] = l0
        acc_ref[...] = a0

    qb = q_scr[...].astype(BF16)
    bias = bias_scr[...]
    cols = bias.shape[1]

    def page_scores(r):
        return lax.dot_general(qb, k_refs[r][...].astype(BF16), NT_DIMS, preferred_element_type=F32) + bias

    def page_values(r):
        return v_refs[r][...].astype(BF16)

    @pl.when(unstabilised)
    def _():
        pv = jnp.zeros((rows, DV_DIFF), F32)
        psum = jnp.zeros((rows, LANES), F32)
        for r in range(pages):
            p = jnp.exp2(page_scores(r))
            for c0 in range(0, cols, LANES):
                psum = psum + p[:, c0:c0 + LANES]
            pv = pv + jnp.dot(p.astype(BF16), page_values(r), preferred_element_type=F32)
        l_ref[...] += jnp.sum(psum, axis=-1, keepdims=True)
        acc_ref[...] += pv

    @pl.when(jnp.logical_not(unstabilised))
    def _():
        s = jnp.concatenate([page_scores(r) for r in range(pages)], axis=1)
        m_prev = m_ref[...]
        m_new = jnp.maximum(m_prev, jnp.max(s, axis=-1, keepdims=True))
        alpha = jnp.exp2(m_prev - m_new)
        p = jnp.exp2(s - m_new)
        l_ref[...] = alpha * l_ref[...] + jnp.sum(p, axis=-1, keepdims=True)
        pb = p.astype(BF16)
        pv = jnp.dot(pb[:, :cols], page_values(0), preferred_element_type=F32)
        for r in range(1, pages):
            pv = pv + jnp.dot(pb[:, r * cols:(r + 1) * cols], page_values(r), preferred_element_type=F32)
        acc_ref[...] = alpha * acc_ref[...] + pv
        m_ref[...] = m_new

    @pl.when(p_step == pl.num_programs(1) - 1)
    def _():
        on = acc_ref[...] / l_ref[...]
        lam = _lambda_value(lam_ref, lam_init)
        gain = gn_ref[...]
        for h in range(H_DIFF):
            o = on[h * hr:h * hr + ts] - lam * on[h * hr + ts:(h + 1) * hr]
            y_ref[:, h * DV_DIFF:(h + 1) * DV_DIFF] = _head_norm(o, gain, lam_init).astype(BF16)


def _diff_sample(z16, cache_k, cache_v, page_table, logit_bound, lam_vecs, gn, batch, ts, lam_init):
    n_pages = page_table.shape[1]
    page_rows = cache_k.shape[1]
    width = H_DIFF * DV_DIFF
    pages = PAGES_PER_STEP
    assert n_pages % pages == 0
    rows = 2 * H_DIFF * ts
    z3 = z16.reshape(batch, ts, N_SECTIONS * SECTION)
    new_spec = lambda sec: pl.BlockSpec((None, ts, SECTION), lambda b, p, pt: (b, 0, sec))

    def page_spec(r):
        return pl.BlockSpec((None, page_rows, DV_DIFF), lambda b, p, pt: (pt[b, p * pages + r], 0, 0))

    grid_spec = pltpu.PrefetchScalarGridSpec(
        num_scalar_prefetch=1,
        grid=(batch, n_pages // pages),
        in_specs=[
            pl.BlockSpec(memory_space=pltpu.SMEM),
            pl.BlockSpec((4, D_HEAD), lambda b, p, pt: (0, 0)),
            new_spec(3), new_spec(4), new_spec(5),
            pl.BlockSpec((1, DV_DIFF), lambda b, p, pt: (0, 0)),
        ] + [page_spec(r) for r in range(pages)] + [page_spec(r) for r in range(pages)],
        out_specs=pl.BlockSpec((None, ts, width), lambda b, p, pt: (b, 0, 0)),
        scratch_shapes=[
            pltpu.VMEM((rows, LANES), F32),
            pltpu.VMEM((rows, page_rows), F32),
            pltpu.VMEM((rows, 1), F32),
            pltpu.VMEM((rows, 1), F32),
            pltpu.VMEM((rows, DV_DIFF), F32),
        ],
    )
    y = pl.pallas_call(
        functools.partial(_diff_sample_kernel, ts=ts, lam_init=lam_init),
        grid_spec=grid_spec,
        out_shape=jax.ShapeDtypeStruct((batch, ts, width), BF16),
        compiler_params=_params(("parallel", "arbitrary")),
        name="diff_attention_sample",
    )(page_table, logit_bound, lam_vecs, z3, z3, z3, gn, *([cache_k] * pages), *([cache_v] * pages))
    return y.reshape(batch * ts, width)


def _prompt_schedule(batch, seq, tq, n_steps):
    nq = seq // tq
    rec = []
    for bp in range(batch):
        for h in range(H_DIFF):
            for qi in range(nq):
                nblk = qi + 1
                groups = -(-nblk // ATTN_SLOTS)
                for g in range(groups):
                    kbs = [g * ATTN_SLOTS + j for j in range(ATTN_SLOTS)]
                    used = min(ATTN_SLOTS, nblk - g * ATTN_SLOTS)
                    rec.append([bp * nq + qi, h, bp]
                               + [min(kb, nblk - 1) for kb in kbs]
                               + [(qi - kb) * tq if j < used else NEVER for j, kb in enumerate(kbs)]
                               + [int(g == 0), int(g == groups - 1), used])
    if len(rec) > n_steps:
        return None
    idle = rec[-1][:3 + ATTN_SLOTS] + [NEVER] * ATTN_SLOTS + [0, 0, 0]
    rec += [idle] * (n_steps - len(rec))
    return np.asarray(rec, np.int32).reshape(-1)


def _fused_attn_kernel(pt_ref, sched_ref, lam_ref, q_ref, kn_ref, vn_ref, gn_ref, pq_ref, pk_ref, pv_ref,
                       *rest, ts, lam_init):
    del pt_ref
    pages = PAGES_PER_STEP
    k_refs = rest[:pages]
    v_refs = rest[pages:2 * pages]
    y_ref, yp_ref, q_scr, bias_scr, l_ref, acc_ref, rel_scr, pacc_ref = rest[2 * pages:]
    p_step = pl.program_id(1)
    step = pl.program_id(0) * pl.num_programs(1) + p_step
    hr = 2 * ts
    rows = H_DIFF * hr
    tq = pq_ref.shape[0]

    def head_cols(x, h):
        return x[:, h * DV_DIFF:(h + 1) * DV_DIFF]

    @pl.when(step == 0)
    def _():
        rel_scr[...] = (lax.broadcasted_iota(jnp.int32, rel_scr.shape, 1)
                        - lax.broadcasted_iota(jnp.int32, rel_scr.shape, 0) % tq)
        pacc_ref[...] = jnp.zeros(pacc_ref.shape, F32)

    @pl.when(p_step == 0)
    def _():
        q32 = q_ref[...].astype(F32)
        r8 = lax.broadcasted_iota(jnp.int32, (hr, LANES), 0)
        lane8 = lax.broadcasted_iota(jnp.int32, (hr, LANES), 1)
        keep = (lane8 // D_HEAD) == (r8 // ts)
        q_heads = []
        for h in range(H_DIFF):
            blk = head_cols(q32, h)
            qt = jnp.zeros((hr, LANES), F32)
            for t in range(ts):
                qt = jnp.where(r8 % ts == t, blk[t:t + 1, :], qt)
            q_heads.append(jnp.where(keep, qt, 0.0))
        q_all = jnp.concatenate(q_heads, axis=0)
        q_scr[...] = q_all
        brow = lax.broadcasted_iota(jnp.int32, bias_scr.shape, 0)
        bcol = lax.broadcasted_iota(jnp.int32, bias_scr.shape, 1)
        bias_scr[...] = jnp.where(bcol % H_DIFF == brow // hr, 0.0, -jnp.inf)

        kn = kn_ref[...].astype(F32)
        vn = vn_ref[...].astype(F32)

        def per_row(x, t):
            return jnp.concatenate(
                [jnp.broadcast_to(head_cols(x, h)[t:t + 1, :], (hr, LANES)) for h in range(H_DIFF)], axis=0)

        qpos = lax.broadcasted_iota(jnp.int32, (rows, 1), 0) % ts
        l0 = jnp.zeros((rows, 1), F32)
        a0 = jnp.zeros((rows, LANES), F32)
        for t in range(ts):
            s_new = jnp.sum(q_all * per_row(kn, t), axis=-1, keepdims=True)
            pt = jnp.where(qpos >= t, jnp.exp2(s_new), 0.0)
            l0 = l0 + pt
            a0 = a0 + pt * per_row(vn, t)
        l_ref[...] = l0
        acc_ref[...] = a0

    qb = q_scr[...].astype(BF16)
    bias = bias_scr[...]
    cols = bias.shape[1]
    pv = jnp.zeros((rows, DV_DIFF), F32)
    psum = jnp.zeros((rows, LANES), F32)
    for r in range(pages):
        s = lax.dot_general(qb, k_refs[r][...].astype(BF16), NT_DIMS, preferred_element_type=F32) + bias
        p = jnp.exp2(s)
        for c0 in range(0, cols, LANES):
            psum = psum + p[:, c0:c0 + LANES]
        pv = pv + jnp.dot(p.astype(BF16), v_refs[r][...].astype(BF16), preferred_element_type=F32)
    l_ref[...] += jnp.sum(psum, axis=-1, keepdims=True)
    acc_ref[...] += pv

    rec = step * SCHED_REC
    used = sched_ref[rec + 5 + 2 * ATTN_SLOTS]

    def prompt_blocks(n):
        q = pq_ref[...]
        lane = lax.broadcasted_iota(jnp.int32, q.shape, 1)
        zero = jnp.zeros_like(q)
        qs = jnp.concatenate([jnp.where(lane < D_HEAD, q, zero), jnp.where(lane >= D_HEAD, q, zero)], axis=0)
        ones = jnp.ones((tq, LANES), BF16)
        rel = rel_scr[...]
        pacc = jnp.where(sched_ref[rec + 3 + 2 * ATTN_SLOTS] == 1, 0.0, pacc_ref[...])
        for j in range(n):
            off = pl.multiple_of(sched_ref[rec + 3 + j] * tq, tq)
            visible = rel <= sched_ref[rec + 3 + ATTN_SLOTS + j]
            s = lax.dot_general(qs, pk_ref[pl.ds(off, tq), :], NT_DIMS, preferred_element_type=F32)
            p = jnp.exp2(jnp.where(visible, s, -jnp.inf)).astype(BF16)
            pacc = pacc + jnp.dot(p, jnp.concatenate([pv_ref[pl.ds(off, tq), :], ones], axis=1),
                                  preferred_element_type=F32)
        pacc_ref[...] = pacc

    for n in range(1, ATTN_SLOTS + 1):
        pl.when(used == n)(functools.partial(prompt_blocks, n))

    @pl.when(p_step == pl.num_programs(1) - 1)
    def _():
        on = acc_ref[...] / l_ref[...]
        lam = _lambda_value(lam_ref, lam_init)
        gain = gn_ref[...]
        for h in range(H_DIFF):
            o = on[h * hr:h * hr + ts] - lam * on[h * hr + ts:(h + 1) * hr]
            y_ref[:, h * DV_DIFF:(h + 1) * DV_DIFF] = _head_norm(o, gain, lam_init).astype(BF16)

    @pl.when(sched_ref[rec + 4 + 2 * ATTN_SLOTS] == 1)
    def _():
        acc = pacc_ref[...]
        on = acc[:, :DV_DIFF] / acc[:, DV_DIFF:]
        lam = _lambda_value(lam_ref, lam_init)
        o = on[:tq] - lam * on[tq:]
        yp_ref[...] = _head_norm(o, gn_ref[...], lam_init).astype(BF16)


def _fused_attention(z16_s, z16_p, cache_k, cache_v, page_table, sched, lam_vecs, gn, batch_s, ts, batch_p, seq,
                     lam_init):
    n_pages = page_table.shape[1]
    page_rows = cache_k.shape[1]
    width = H_DIFF * DV_DIFF
    pages = PAGES_PER_STEP
    n_steps = n_pages // pages
    rows = 2 * H_DIFF * ts
    tq = min(ATTN_TILE, seq)
    q_col = 3 * SECTION // LANES
    k_col = 4 * SECTION // LANES
    v_col = 5 * SECTION // LANES
    z3 = z16_s.reshape(batch_s, ts, N_SECTIONS * SECTION)
    new_spec = lambda sec: pl.BlockSpec((None, ts, SECTION), lambda b, p, pt, sc: (b, 0, sec))

    def field(b, p, sc, i):
        return sc[(b * n_steps + p) * SCHED_REC + i]

    def page_spec(r):
        return pl.BlockSpec((None, page_rows, DV_DIFF), lambda b, p, pt, sc: (pt[b, p * pages + r], 0, 0))

    grid_spec = pltpu.PrefetchScalarGridSpec(
        num_scalar_prefetch=2,
        grid=(batch_s, n_steps),
        in_specs=[
            pl.BlockSpec((4, D_HEAD), lambda b, p, pt, sc: (0, 0)),
            new_spec(3), new_spec(4), new_spec(5),
            pl.BlockSpec((1, DV_DIFF), lambda b, p, pt, sc: (0, 0)),
            pl.BlockSpec((tq, LANES), lambda b, p, pt, sc: (field(b, p, sc, 0), q_col + field(b, p, sc, 1))),
            pl.BlockSpec((seq, LANES), lambda b, p, pt, sc: (field(b, p, sc, 2), k_col + field(b, p, sc, 1))),
            pl.BlockSpec((seq, LANES), lambda b, p, pt, sc: (field(b, p, sc, 2), v_col + field(b, p, sc, 1))),
        ] + [page_spec(r) for r in range(pages)] + [page_spec(r) for r in range(pages)],
        out_specs=[
            pl.BlockSpec((None, ts, width), lambda b, p, pt, sc: (b, 0, 0)),
            pl.BlockSpec((tq, DV_DIFF), lambda b, p, pt, sc: (field(b, p, sc, 0), field(b, p, sc, 1))),
        ],
        scratch_shapes=[
            pltpu.VMEM((rows, LANES), F32),
            pltpu.VMEM((rows, page_rows), F32),
            pltpu.VMEM((rows, 1), F32),
            pltpu.VMEM((rows, DV_DIFF), F32),
            pltpu.VMEM((2 * tq, tq), jnp.int32),
            pltpu.VMEM((2 * tq, 2 * DV_DIFF), F32),
        ],
    )
    y_s, y_p = pl.pallas_call(
        functools.partial(_fused_attn_kernel, ts=ts, lam_init=lam_init),
        grid_spec=grid_spec,
        out_shape=[jax.ShapeDtypeStruct((batch_s, ts, width), BF16),
                   jax.ShapeDtypeStruct((batch_p * seq, width), BF16)],
        compiler_params=_params(("arbitrary", "arbitrary")),
        name="fused_attention",
    )(page_table, sched, lam_vecs, z3, z3, z3, gn, z16_p, z16_p, z16_p,
      *([cache_k] * pages), *([cache_v] * pages))
    return y_s.reshape(batch_s * ts, width), y_p


def _merge_kernel(x_ref, yr_ref, yd_ref, gr_ref, gd_ref, wr_ref, wd_ref, wo_ref, g_ref, x1_ref, h_ref):
    a = jnp.dot(yr_ref[...], wr_ref[...], preferred_element_type=F32)
    b = jnp.dot(yd_ref[...], wd_ref[...], preferred_element_type=F32)
    m = gr_ref[...].astype(F32) * a + gd_ref[...].astype(F32) * b
    x1 = x_ref[...] + jnp.dot(m.astype(BF16), wo_ref[...], preferred_element_type=F32)
    x1_ref[...] = x1
    ms = jnp.mean(x1 * x1, axis=-1, keepdims=True)
    h_ref[...] = (x1 * lax.rsqrt(ms + EPS) * g_ref[...]).astype(BF16)


def _merge(x, y_r, y_d, z16, w_br_ret, w_br_diff, w_out, g_ffn):
    n = x.shape[0]
    tm = min(TOKEN_TILE, n)
    tile = lambda col: pl.BlockSpec((tm, D_MODEL), lambda i: (i, col))
    weight = pl.BlockSpec((D_MODEL, D_MODEL), lambda i: (0, 0))
    return pl.pallas_call(
        _merge_kernel,
        grid=(n // tm,),
        in_specs=[tile(0), tile(0), tile(0), tile(6), tile(7), weight, weight, weight,
                  pl.BlockSpec((1, D_MODEL), lambda i: (0, 0))],
        out_specs=[tile(0), tile(0)],
        out_shape=[jax.ShapeDtypeStruct((n, D_MODEL), F32), jax.ShapeDtypeStruct((n, D_MODEL), BF16)],
        compiler_params=_params(("parallel",)),
        name="merge",
    )(x, y_r, y_d, z16, z16, w_br_ret, w_br_diff, w_out, g_ffn)


def _ffn_chunks(d_ff):
    edges = list(range(0, d_ff, SECTION)) + [d_ff]
    return list(zip(edges[:-1], edges[1:]))


def _ffn_kernel(x_ref, h_ref, wg_ref, wu_ref, wd_ref, o_ref):
    h = h_ref[...]
    acc = x_ref[...]
    for c0, c1 in _ffn_chunks(wg_ref.shape[1]):
        g = jnp.dot(h, wg_ref[:, c0:c1], preferred_element_type=F32)
        u = jnp.dot(h, wu_ref[:, c0:c1], preferred_element_type=F32)
        a = (g * _sigmoid(g) * u).astype(BF16)
        acc = acc + jnp.dot(a, wd_ref[c0:c1, :], preferred_element_type=F32)
    o_ref[...] = acc


def _ffn(x1, h, wg, wu, wd):
    n = x1.shape[0]
    d_ff = wg.shape[1]
    tm = min(TOKEN_TILE, n)
    tile = pl.BlockSpec((tm, D_MODEL), lambda i: (i, 0))
    single = pl.Buffered(1)
    return pl.pallas_call(
        _ffn_kernel,
        grid=(n // tm,),
        in_specs=[tile, tile,
                  pl.BlockSpec((D_MODEL, d_ff), lambda i: (0, 0), pipeline_mode=single),
                  pl.BlockSpec((D_MODEL, d_ff), lambda i: (0, 0), pipeline_mode=single),
                  pl.BlockSpec((d_ff, D_MODEL), lambda i: (0, 0), pipeline_mode=single)],
        out_specs=tile,
        out_shape=jax.ShapeDtypeStruct((n, D_MODEL), F32),
        compiler_params=_params(("parallel",)),
        name="ffn",
    )(x1, h, wg, wu, wd)


def _rope_tables(pos, half):
    inv = 1.0 / jnp.power(ROPE_THETA, jnp.arange(half, dtype=F32) / half)
    ang = pos.astype(F32)[:, None] * inv[None, :]
    cos = jnp.cos(ang)
    sin = jnp.sin(ang)
    reps = LANES // (2 * half)
    return (jnp.tile(jnp.concatenate([cos, cos], axis=-1), (1, reps)),
            jnp.tile(jnp.concatenate([-sin, sin], axis=-1), (1, reps)))


def _layer_tables(pos):
    cr, sr = _rope_tables(pos, DK_RET // 2)
    cd, sd = _rope_tables(pos, D_HEAD // 2)
    return (cr, sr, cd, sd)


def kernel(x_prompt, x_sample, cache_k, cache_v, state_ret, page_table, norm_mix_g, w_in, ret_gn_g, diff_qnorm_g, diff_knorm_g, lambda_q1, lambda_k1, lambda_q2, lambda_k2, diff_gn_g, w_br_ret, w_br_diff, w_out, norm_ffn_g, w_ffn_gate, w_ffn_up, w_ffn_down):
    B, T, D = x_prompt.shape
    Bs, Ts, _ = x_sample.shape
    depth, n_phys, page = cache_k.shape[:3]
    past_len = page_table.shape[1] * page

    tabs_p = _layer_tables(jnp.arange(T, dtype=jnp.int32))
    tabs_s = _layer_tables(jnp.tile(past_len + jnp.arange(Ts, dtype=jnp.int32), Bs))
    lg = jnp.log(1.0 - jnp.power(2.0, -5.0 - jnp.arange(H_RET, dtype=F32)))
    lg_tab = jnp.broadcast_to(lg[:, None, None], (H_RET, 1, LANES))
    gidx = jnp.arange(NORM_GROUP_TILE) // D_HEAD
    bd = (gidx[:, None] == gidx[None, :]).astype(BF16)
    ck = cache_k.reshape(depth * n_phys, page * H_DIFF, DV_DIFF)
    cv = cache_v.reshape(depth * n_phys, page * H_DIFF, DV_DIFF)

    sched = None
    if min(ATTN_TILE, T) == min(ATTN_KV_TILE, T) and page_table.shape[1] % PAGES_PER_STEP == 0:
        sched = _prompt_schedule(B, T, min(ATTN_TILE, T), Bs * (page_table.shape[1] // PAGES_PER_STEP))
    xp = x_prompt.reshape(B * T, D)
    xs = x_sample.reshape(Bs * Ts, D)
    outs = [[] for _ in range(6)]
    for l in range(depth):
        lam_init = 0.8 - 0.6 * math.exp(-0.3 * l)
        w_in_l = w_in[l].astype(BF16)
        g_mix = norm_mix_g[l].reshape(1, D)
        g_ffn = norm_ffn_g[l].reshape(1, D)
        qg = jnp.tile(diff_qnorm_g[l], LANES // D_HEAD).reshape(1, LANES)
        kg = jnp.tile(diff_knorm_g[l], LANES // D_HEAD).reshape(1, LANES)
        gn_r = ret_gn_g[l].reshape(H_RET, 1, DV_RET)
        gn_d = diff_gn_g[l].reshape(1, DV_DIFF)
        logit_bound = (1.02 * LOG2E * D_HEAD ** 0.5 * jnp.max(jnp.abs(diff_qnorm_g[l]))
                       * jnp.max(jnp.abs(diff_knorm_g[l]))).astype(F32).reshape(1)
        lam_vecs = jnp.stack([lambda_q1[l], lambda_k1[l], lambda_q2[l], lambda_k2[l]]).astype(F32)
        wr = w_br_ret[l].astype(BF16)
        wdf = w_br_diff[l].astype(BF16)
        wo = w_out[l].astype(BF16)
        wg = w_ffn_gate[l].astype(BF16)
        wu = w_ffn_up[l].astype(BF16)
        wd = w_ffn_down[l].astype(BF16)

        zp, k32_p, v32_p = _in_projection(xp, g_mix, w_in_l, tabs_p, qg, kg, bd, T)
        zs, k32_s, v32_s = _in_projection(xs, g_mix, w_in_l, tabs_s, qg, kg, bd, Bs * Ts)
        yr_p, s_p = _retention_prompt(zp, lg_tab, gn_r, jnp.zeros((B, H_RET, DK_RET, DV_RET), F32), B, T)
        yr_s, s_s = _retention_sample(zs, lg_tab, gn_r, state_ret[l], Bs, Ts)
        pt_l = page_table + l * n_phys

        def separate_attention():
            return (_diff_sample(zs, ck, cv, pt_l, logit_bound, lam_vecs, gn_d, Bs, Ts, lam_init),
                    _diff_prompt(zp, logit_bound, lam_vecs, gn_d, B, T, lam_init))

        def fused_attention():
            return _fused_attention(zs, zp, ck, cv, pt_l, jnp.asarray(sched), lam_vecs, gn_d, Bs, Ts, B, T, lam_init)

        if sched is None:
            yd_s, yd_p = separate_attention()
        else:
            yd_s, yd_p = lax.cond(logit_bound[0] <= SAFE_LOGIT_BOUND, fused_attention, separate_attention)

        x1, h2 = _merge(xp, yr_p, yd_p, zp, wr, wdf, wo, g_ffn)
        xp = _ffn(x1, h2, wg, wu, wd)
        x1, h2 = _merge(xs, yr_s, yd_s, zs, wr, wdf, wo, g_ffn)
        xs = _ffn(x1, h2, wg, wu, wd)
        outs[0].append(k32_p.reshape(B, T, H_DIFF, 2 * D_HEAD))
        outs[1].append(v32_p.reshape(B, T, H_DIFF, DV_DIFF))
        outs[2].append(s_p)
        outs[3].append(k32_s.reshape(Bs, Ts, H_DIFF, 2 * D_HEAD))
        outs[4].append(v32_s.reshape(Bs, Ts, H_DIFF, DV_DIFF))
        outs[5].append(s_s)

    stacked = [jnp.stack(o) for o in outs]
    return (xp.reshape(B, T, D), xs.reshape(Bs, Ts, D),
            stacked[0], stacked[1], stacked[2], stacked[3], stacked[4], stacked[5])
```

```python
import functools
import math

import jax
import jax.numpy as jnp
import numpy as np
from jax import lax
from jax.experimental import pallas as pl
from jax.experimental.pallas import tpu as pltpu

F32 = jnp.float32
BF16 = jnp.bfloat16

D_MODEL = 1024
H_RET = 4
DK_RET = 128
DV_RET = 256
H_DIFF = 8
D_HEAD = 64
DV_DIFF = 2 * D_HEAD
ROPE_THETA = 10000.0
EPS = 1e-6
LOG2E = 1.4426950408889634
SAFE_LOGIT_BOUND = 64.0

LANES = 128
SECTION = 1024
N_SECTIONS = 8
NORM_GROUP_TILE = 256
VMEM_LIMIT = 56 * 1024 * 1024

TOKEN_TILE = 512
RET_CHUNK = 512
ATTN_TILE = 512
ATTN_KV_TILE = 512
PAGES_PER_STEP = 8
RET_SAMPLE_GROUP = 8
ATTN_SLOTS = 3
SCHED_REC = 12
NEVER = -(1 << 30)

NT_DIMS = (((1,), (1,)), ((), ()))
TN_DIMS = (((0,), (0,)), ((), ()))


def _sigmoid(x):
    return 1.0 / (1.0 + jnp.exp(-x))


def _params(semantics):
    return pltpu.CompilerParams(dimension_semantics=semantics, vmem_limit_bytes=VMEM_LIMIT)


def _inproj_kernel(x_ref, g_ref, w_ref, cr_ref, sr_ref, cd_ref, sd_ref, qg_ref, kg_ref, bd_ref,
                   z_ref, k32_ref, v32_ref):
    x = x_ref[...]
    ms = jnp.mean(x * x, axis=-1, keepdims=True)
    h = (x * lax.rsqrt(ms + EPS) * g_ref[...]).astype(BF16)

    def project(j):
        return jnp.dot(h, w_ref[:, j * SECTION:(j + 1) * SECTION], preferred_element_type=F32)

    def put(j, col, width, val):
        z_ref[:, j * SECTION + col:j * SECTION + col + width] = val.astype(BF16)

    acc = project(0)
    c = cr_ref[...]
    s = sr_ref[...]
    for hh in range(SECTION // LANES):
        blk = acc[:, hh * LANES:(hh + 1) * LANES]
        r = blk * c + pltpu.roll(blk, DK_RET // 2, 1) * s
        if hh >= H_RET:
            r = r * (DK_RET ** -0.5)
        put(0, hh * LANES, LANES, r)

    put(1, 0, SECTION, project(1))
    acc = project(2)
    put(2, 0, SECTION, acc * _sigmoid(acc))

    def qk_norm_rope(acc, gain, scale, emit):
        c = cd_ref[...]
        s = sd_ref[...]
        lane = lax.broadcasted_iota(jnp.int32, c.shape, 1)
        first_half = (lane % D_HEAD) < (D_HEAD // 2)
        for t in range(SECTION // NORM_GROUP_TILE):
            blk = acc[:, t * NORM_GROUP_TILE:(t + 1) * NORM_GROUP_TILE]
            ms = jnp.dot((blk * blk).astype(BF16), bd_ref[...], preferred_element_type=F32) * (1.0 / D_HEAD)
            y = blk * lax.rsqrt(ms + EPS)
            for u in range(NORM_GROUP_TILE // LANES):
                yb = y[:, u * LANES:(u + 1) * LANES] * gain
                partner = jnp.where(first_half,
                                    pltpu.roll(yb, LANES - D_HEAD // 2, 1),
                                    pltpu.roll(yb, D_HEAD // 2, 1))
                r = yb * c + partner * s
                if scale != 1.0:
                    r = r * scale
                emit(t * NORM_GROUP_TILE + u * LANES, r)

    qk_norm_rope(project(3), qg_ref[...], D_HEAD ** -0.5 * LOG2E, lambda col, r: put(3, col, LANES, r))

    def emit_k(col, r):
        k32_ref[:, col // LANES, :] = r
        put(4, col, LANES, r)
    qk_norm_rope(project(4), kg_ref[...], 1.0, emit_k)

    acc = project(5)
    for hh in range(H_DIFF):
        v32_ref[:, hh, :] = acc[:, hh * DV_DIFF:(hh + 1) * DV_DIFF]
    put(5, 0, SECTION, acc)

    for j in (6, 7):
        put(j, 0, SECTION, _sigmoid(project(j)))


def _in_projection(x, g, w_bf16, tabs, qg, kg, bd, table_rows):
    n = x.shape[0]
    tm = min(TOKEN_TILE, n)
    n_tab_blocks = table_rows // tm
    tab_spec = pl.BlockSpec((tm, LANES), lambda i: (i % n_tab_blocks, 0))
    const = lambda shape: pl.BlockSpec(shape, lambda i: (0, 0))
    return pl.pallas_call(
        _inproj_kernel,
        grid=(n // tm,),
        in_specs=[
            pl.BlockSpec((tm, D_MODEL), lambda i: (i, 0)),
            const((1, D_MODEL)),
            pl.BlockSpec((D_MODEL, N_SECTIONS * SECTION), lambda i: (0, 0), pipeline_mode=pl.Buffered(1)),
            tab_spec, tab_spec, tab_spec, tab_spec,
            const((1, LANES)), const((1, LANES)),
            const((NORM_GROUP_TILE, NORM_GROUP_TILE)),
        ],
        out_specs=[
            pl.BlockSpec((tm, N_SECTIONS * SECTION), lambda i: (i, 0)),
            pl.BlockSpec((tm, H_DIFF, DV_DIFF), lambda i: (i, 0, 0)),
            pl.BlockSpec((tm, H_DIFF, DV_DIFF), lambda i: (i, 0, 0)),
        ],
        out_shape=[
            jax.ShapeDtypeStruct((n, N_SECTIONS * SECTION), BF16),
            jax.ShapeDtypeStruct((n, H_DIFF, DV_DIFF), F32),
            jax.ShapeDtypeStruct((n, H_DIFF, DV_DIFF), F32),
        ],
        compiler_params=_params(("parallel",)),
        name="in_projection",
    )(x, g, w_bf16, *tabs, qg, kg, bd)


def _retention_kernel(lg_ref, q_ref, k_ref, v_ref, sg_ref, gn_ref, s0_ref, y_ref, s_ref, decay_ref):
    c = pl.program_id(1)
    chunk = q_ref.shape[0]

    @pl.when(c == 0)
    def _():
        s_ref[...] = s0_ref[...]
        row = lax.broadcasted_iota(jnp.int32, (chunk, chunk), 0)
        col = lax.broadcasted_iota(jnp.int32, (chunk, chunk), 1)
        dist = (row - col).astype(F32)
        for h in range(H_RET):
            lg = lg_ref[h][:, :1]
            decay_ref[h] = jnp.where(dist >= 0, jnp.exp(jnp.maximum(dist, 0.0) * lg), 0.0)

    idx = lax.broadcasted_iota(jnp.int32, (chunk, 1), 0).astype(F32)
    for h in range(H_RET):
        lg = lg_ref[h][:, :1]
        xi = jnp.exp((idx + 1.0) * lg)
        zeta = jnp.exp((chunk - 1.0 - idx) * lg)
        chunk_decay = jnp.exp(chunk * lg)
        q = q_ref[:, h * DK_RET:(h + 1) * DK_RET]
        k = k_ref[:, h * DK_RET:(h + 1) * DK_RET]
        v = v_ref[:, h * DV_RET:(h + 1) * DV_RET]
        state = s_ref[h]
        att = lax.dot_general(q, k, NT_DIMS, preferred_element_type=F32) * decay_ref[h]
        o = (jnp.dot(att.astype(BF16), v, preferred_element_type=F32)
             + jnp.dot(q, state.astype(BF16), preferred_element_type=F32) * xi)
        kz = (k.astype(F32) * zeta).astype(BF16)
        s_ref[h] = chunk_decay * state + lax.dot_general(kz, v, TN_DIMS, preferred_element_type=F32)
        ms = jnp.mean(o * o, axis=-1, keepdims=True)
        y = o * lax.rsqrt(ms + EPS) * gn_ref[h] * sg_ref[:, h * DV_RET:(h + 1) * DV_RET].astype(F32)
        y_ref[:, h * DV_RET:(h + 1) * DV_RET] = y.astype(BF16)


def _retention_prompt(z16, lg_tab, gn, s0, batch, seq):
    chunk = min(RET_CHUNK, seq)
    n_chunks = seq // chunk
    row = lambda b, c: b * n_chunks + c
    return pl.pallas_call(
        _retention_kernel,
        grid=(batch, n_chunks),
        in_specs=[
            pl.BlockSpec((H_RET, 1, LANES), lambda b, c: (0, 0, 0)),
            pl.BlockSpec((chunk, H_RET * DK_RET), lambda b, c: (row(b, c), 0)),
            pl.BlockSpec((chunk, H_RET * DK_RET), lambda b, c: (row(b, c), 1)),
            pl.BlockSpec((chunk, H_RET * DV_RET), lambda b, c: (row(b, c), 1)),
            pl.BlockSpec((chunk, H_RET * DV_RET), lambda b, c: (row(b, c), 2)),
            pl.BlockSpec((H_RET, 1, DV_RET), lambda b, c: (0, 0, 0)),
            pl.BlockSpec((None, H_RET, DK_RET, DV_RET), lambda b, c: (b, 0, 0, 0)),
        ],
        out_specs=[
            pl.BlockSpec((chunk, H_RET * DV_RET), lambda b, c: (row(b, c), 0)),
            pl.BlockSpec((None, H_RET, DK_RET, DV_RET), lambda b, c: (b, 0, 0, 0)),
        ],
        out_shape=[
            jax.ShapeDtypeStruct((batch * seq, H_RET * DV_RET), BF16),
            jax.ShapeDtypeStruct((batch, H_RET, DK_RET, DV_RET), F32),
        ],
        scratch_shapes=[pltpu.VMEM((H_RET, chunk, chunk), F32)],
        compiler_params=_params(("parallel", "arbitrary")),
        name="retention_prompt",
    )(lg_tab, z16, z16, z16, z16, gn, s0)


def _retention_sample_kernel(lg_ref, q_ref, k_ref, v_ref, sg_ref, gn_ref, s0_ref, y_ref, s_ref, *, ts):
    rows = q_ref.shape[0]
    group = rows // ts
    row = lax.broadcasted_iota(jnp.int32, (rows, rows), 0)
    col = lax.broadcasted_iota(jnp.int32, (rows, rows), 1)
    same_seq = (row // ts) == (col // ts)
    dist = ((row % ts) - (col % ts)).astype(F32)
    ridx = lax.broadcasted_iota(jnp.int32, (rows, 1), 0)
    tpos = (ridx % ts).astype(F32)
    seq_of_row = ridx // ts

    for h in range(H_RET):
        lg = lg_ref[h][:, :1]
        decay = jnp.where(same_seq & (dist >= 0), jnp.exp(jnp.maximum(dist, 0.0) * lg), 0.0)
        xi = jnp.exp((tpos + 1.0) * lg)
        zeta = jnp.exp((ts - 1.0 - tpos) * lg)
        seq_decay = jnp.exp(ts * lg)
        q = q_ref[:, h * DK_RET:(h + 1) * DK_RET]
        k = k_ref[:, h * DK_RET:(h + 1) * DK_RET]
        v = v_ref[:, h * DV_RET:(h + 1) * DV_RET]
        att = lax.dot_general(q, k, NT_DIMS, preferred_element_type=F32) * decay
        o = jnp.dot(att.astype(BF16), v, preferred_element_type=F32)
        kz = k.astype(F32) * zeta
        for bb in range(group):
            mine = seq_of_row == bb
            state = s0_ref[bb, h]
            full = jnp.dot(q, state.astype(BF16), preferred_element_type=F32) * xi
            o = o + jnp.where(mine, full, 0.0)
            kz_b = jnp.where(mine, kz, 0.0).astype(BF16)
            s_ref[bb, h] = seq_decay * state + lax.dot_general(kz_b, v, TN_DIMS, preferred_element_type=F32)
        ms = jnp.mean(o * o, axis=-1, keepdims=True)
        y = o * lax.rsqrt(ms + EPS) * gn_ref[h] * sg_ref[:, h * DV_RET:(h + 1) * DV_RET].astype(F32)
        y_ref[:, h * DV_RET:(h + 1) * DV_RET] = y.astype(BF16)


def _retention_sample(z16, lg_tab, gn, s0, batch, ts):
    group = min(RET_SAMPLE_GROUP, batch)
    rows = group * ts
    return pl.pallas_call(
        functools.partial(_retention_sample_kernel, ts=ts),
        grid=(batch // group,),
        in_specs=[
            pl.BlockSpec((H_RET, 1, LANES), lambda i: (0, 0, 0)),
            pl.BlockSpec((rows, H_RET * DK_RET), lambda i: (i, 0)),
            pl.BlockSpec((rows, H_RET * DK_RET), lambda i: (i, 1)),
            pl.BlockSpec((rows, H_RET * DV_RET), lambda i: (i, 1)),
            pl.BlockSpec((rows, H_RET * DV_RET), lambda i: (i, 2)),
            pl.BlockSpec((H_RET, 1, DV_RET), lambda i: (0, 0, 0)),
            pl.BlockSpec((group, H_RET, DK_RET, DV_RET), lambda i: (i, 0, 0, 0)),
        ],
        out_specs=[
            pl.BlockSpec((rows, H_RET * DV_RET), lambda i: (i, 0)),
            pl.BlockSpec((group, H_RET, DK_RET, DV_RET), lambda i: (i, 0, 0, 0)),
        ],
        out_shape=[
            jax.ShapeDtypeStruct((batch * ts, H_RET * DV_RET), BF16),
            jax.ShapeDtypeStruct((batch, H_RET, DK_RET, DV_RET), F32),
        ],
        compiler_params=_params(("parallel",)),
        name="retention_sample",
    )(lg_tab, z16, z16, z16, z16, gn, s0)


def _lambda_value(lam_ref, lam_init):
    lv = lam_ref[...]
    a = jnp.sum(lv[0:1] * lv[1:2], axis=-1, keepdims=True)
    b = jnp.sum(lv[2:3] * lv[3:4], axis=-1, keepdims=True)
    return jnp.exp(a) - jnp.exp(b) + lam_init


def _head_norm(o, gain, lam_init):
    ms = jnp.mean(o * o, axis=-1, keepdims=True)
    return o * lax.rsqrt(ms + EPS) * gain * (1.0 - lam_init)


def _diff_prompt_kernel(bound_ref, lam_ref, q_ref, k_ref, v_ref, gn_ref, y_ref, acc_ref, *, lam_init, tk):
    qi = pl.program_id(2)
    tq = q_ref.shape[0]
    rows = 2 * tq
    q = q_ref[...]
    lane = lax.broadcasted_iota(jnp.int32, q.shape, 1)
    zero = jnp.zeros_like(q)
    qs = jnp.concatenate([jnp.where(lane < D_HEAD, q, zero), jnp.where(lane >= D_HEAD, q, zero)], axis=0)
    ones = jnp.ones((tk, LANES), BF16)
    acc_ref[...] = jnp.zeros(acc_ref.shape, F32)

    def lanes(x, n):
        return jnp.concatenate([x] * (n // LANES), axis=1)

    def scores(kb):
        k = k_ref[pl.ds(pl.multiple_of(kb * tk, tk), tk), :]
        return lax.dot_general(qs, k, NT_DIMS, preferred_element_type=F32)

    def values(kb):
        v = v_ref[pl.ds(pl.multiple_of(kb * tk, tk), tk), :]
        return jnp.concatenate([v, ones], axis=1)

    def causal(s):
        qpos = qi * tq + lax.broadcasted_iota(jnp.int32, s.shape, 0) % tq
        kpos = n_full * tk + lax.broadcasted_iota(jnp.int32, s.shape, 1)
        return jnp.where(kpos <= qpos, s, -jnp.inf)

    n_full = (qi * tq) // tk

    unstabilised = bound_ref[0] <= SAFE_LOGIT_BOUND

    def weights(kb, masked):
        s = scores(kb)
        return jnp.exp2(causal(s) if masked else s).astype(BF16)

    def accumulate(kb, p):
        acc_ref[...] += jnp.dot(p, values(kb), preferred_element_type=F32)

    @pl.when(unstabilised & (n_full == 0))
    def _():
        accumulate(0, weights(0, True))

    @pl.when(unstabilised & (n_full > 0))
    def _():
        def body(kb, p):
            p_next = weights(kb + 1, False)
            accumulate(kb, p)
            return p_next

        p = lax.fori_loop(0, n_full - 1, body, weights(0, False))
        p_last = weights(n_full, True)
        accumulate(n_full - 1, p)
        accumulate(n_full, p_last)

    @pl.when(bound_ref[0] > SAFE_LOGIT_BOUND)
    def _():
        def update(kb, s, m_prev):
            m_blk = jnp.max(s, axis=-1, keepdims=True)
            m_new = jnp.maximum(m_prev, jnp.broadcast_to(m_blk, (rows, LANES)))
            alpha = jnp.exp2(m_prev - m_new)
            p = jnp.exp2(s - lanes(m_new, tk)).astype(BF16)
            pv = jnp.dot(p, values(kb), preferred_element_type=F32)
            acc_ref[...] = lanes(alpha, 2 * LANES) * acc_ref[...] + pv
            return m_new

        m = lax.fori_loop(0, n_full, lambda kb, m_prev: update(kb, scores(kb), m_prev),
                          jnp.full((rows, LANES), -jnp.inf, F32))
        update(n_full, causal(scores(n_full)), m)

    acc = acc_ref[...]
    on = acc[:, :DV_DIFF] / acc[:, DV_DIFF:]
    lam = _lambda_value(lam_ref, lam_init)
    o = on[:tq] - lam * on[tq:]
    y_ref[...] = _head_norm(o, gn_ref[...], lam_init).astype(BF16)


def _diff_prompt(z16, logit_bound, lam_vecs, gn, batch, seq, lam_init):
    tq = min(ATTN_TILE, seq)
    tk = min(ATTN_KV_TILE, seq)
    nq = seq // tq
    q_col = 3 * SECTION // LANES
    k_col = 4 * SECTION // LANES
    v_col = 5 * SECTION // LANES
    return pl.pallas_call(
        functools.partial(_diff_prompt_kernel, lam_init=lam_init, tk=tk),
        grid=(batch, H_DIFF, nq),
        in_specs=[
            pl.BlockSpec(memory_space=pltpu.SMEM),
            pl.BlockSpec((4, D_HEAD), lambda b, h, i: (0, 0)),
            pl.BlockSpec((tq, LANES), lambda b, h, i: (b * nq + i, q_col + h)),
            pl.BlockSpec((seq, LANES), lambda b, h, i: (b, k_col + h)),
            pl.BlockSpec((seq, LANES), lambda b, h, i: (b, v_col + h)),
            pl.BlockSpec((1, DV_DIFF), lambda b, h, i: (0, 0)),
        ],
        out_specs=pl.BlockSpec((tq, DV_DIFF), lambda b, h, i: (b * nq + i, h)),
        out_shape=jax.ShapeDtypeStruct((batch * seq, H_DIFF * DV_DIFF), BF16),
        scratch_shapes=[pltpu.VMEM((2 * tq, 2 * DV_DIFF), F32)],
        compiler_params=_params(("parallel", "parallel", "arbitrary")),
        name="diff_attention_prompt",
    )(logit_bound, lam_vecs, z16, z16, z16, gn)


def _diff_sample_kernel(pt_ref, bound_ref, lam_ref, q_ref, kn_ref, vn_ref, gn_ref, *rest, ts, lam_init):
    del pt_ref
    pages = PAGES_PER_STEP
    k_refs = rest[:pages]
    v_refs = rest[pages:2 * pages]
    y_ref, q_scr, bias_scr, m_ref, l_ref, acc_ref = rest[2 * pages:]
    p_step = pl.program_id(1)
    hr = 2 * ts
    rows = H_DIFF * hr
    unstabilised = bound_ref[0] <= SAFE_LOGIT_BOUND

    def head_cols(x, h):
        return x[:, h * DV_DIFF:(h + 1) * DV_DIFF]

    @pl.when(p_step == 0)
    def _():
        q32 = q_ref[...].astype(F32)
        r8 = lax.broadcasted_iota(jnp.int32, (hr, LANES), 0)
        lane8 = lax.broadcasted_iota(jnp.int32, (hr, LANES), 1)
        keep = (lane8 // D_HEAD) == (r8 // ts)
        q_heads = []
        for h in range(H_DIFF):
            blk = head_cols(q32, h)
            qt = jnp.zeros((hr, LANES), F32)
            for t in range(ts):
                qt = jnp.where(r8 % ts == t, blk[t:t + 1, :], qt)
            q_heads.append(jnp.where(keep, qt, 0.0))
        q_all = jnp.concatenate(q_heads, axis=0)
        q_scr[...] = q_all
        brow = lax.broadcasted_iota(jnp.int32, bias_scr.shape, 0)
        bcol = lax.broadcasted_iota(jnp.int32, bias_scr.shape, 1)
        bias_scr[...] = jnp.where(bcol % H_DIFF == brow // hr, 0.0, -jnp.inf)

        kn = kn_ref[...].astype(F32)
        vn = vn_ref[...].astype(F32)

        def per_row(x, t):
            return jnp.concatenate(
                [jnp.broadcast_to(head_cols(x, h)[t:t + 1, :], (hr, LANES)) for h in range(H_DIFF)], axis=0)

        qpos = lax.broadcasted_iota(jnp.int32, (rows, 1), 0) % ts
        s_new = [jnp.sum(q_all * per_row(kn, t), axis=-1, keepdims=True) for t in range(ts)]
        m0 = s_new[0]
        for t in range(1, ts):
            m0 = jnp.where(qpos >= t, jnp.maximum(m0, s_new[t]), m0)
        m0 = jnp.where(unstabilised, 0.0, m0)
        l0 = jnp.zeros((rows, 1), F32)
        a0 = jnp.zeros((rows, LANES), F32)
        for t in range(ts):
            pt = jnp.where(qpos >= t, jnp.exp2(s_new[t] - m0), 0.0)
            l0 = l0 + pt
            a0 = a0 + pt * per_row(vn, t)
        m_ref[...] = m0
        l_ref[...] = l0
        acc_ref[...] = a0

    qb = q_scr[...].astype(BF16)
    bias = bias_scr[...]
    cols = bias.shape[1]

    def page_scores(r):
        return lax.dot_general(qb, k_refs[r][...].astype(BF16), NT_DIMS, preferred_element_type=F32) + bias

    def page_values(r):
        return v_refs[r][...].astype(BF16)

    @pl.when(unstabilised)
    def _():
        pv = jnp.zeros((rows, DV_DIFF), F32)
        psum = jnp.zeros((rows, LANES), F32)
        for r in range(pages):
            p = jnp.exp2(page_scores(r))
            for c0 in range(0, cols, LANES):
                psum = psum + p[:, c0:c0 + LANES]
            pv = pv + jnp.dot(p.astype(BF16), page_values(r), preferred_element_type=F32)
        l_ref[...] += jnp.sum(psum, axis=-1, keepdims=True)
        acc_ref[...] += pv

    @pl.when(jnp.logical_not(unstabilised))
    def _():
        s = jnp.concatenate([page_scores(r) for r in range(pages)], axis=1)
        m_prev = m_ref[...]
        m_new = jnp.maximum(m_prev, jnp.max(s, axis=-1, keepdims=True))
        alpha = jnp.exp2(m_prev - m_new)
        p = jnp.exp2(s - m_new)
        l_ref[...] = alpha * l_ref[...] + jnp.sum(p, axis=-1, keepdims=True)
        pb = p.astype(BF16)
        pv = jnp.dot(pb[:, :cols], page_values(0), preferred_element_type=F32)
        for r in range(1, pages):
            pv = pv + jnp.dot(pb[:, r * cols:(r + 1) * cols], page_values(r), preferred_element_type=F32)
        acc_ref[...] = alpha * acc_ref[...] + pv
        m_ref[...] = m_new

    @pl.when(p_step == pl.num_programs(1) - 1)
    def _():
        on = acc_ref[...] / l_ref[...]
        lam = _lambda_value(lam_ref, lam_init)
        gain = gn_ref[...]
        for h in range(H_DIFF):
            o = on[h * hr:h * hr + ts] - lam * on[h * hr + ts:(h + 1) * hr]
            y_ref[:, h * DV_DIFF:(h + 1) * DV_DIFF] = _head_norm(o, gain, lam_init).astype(BF16)


def _diff_sample(z16, cache_k, cache_v, page_table, logit_bound, lam_vecs, gn, batch, ts, lam_init):
    n_pages = page_table.shape[1]
    page_rows = cache_k.shape[1]
    width = H_DIFF * DV_DIFF
    pages = PAGES_PER_STEP
    assert n_pages % pages == 0
    rows = 2 * H_DIFF * ts
    z3 = z16.reshape(batch, ts, N_SECTIONS * SECTION)
    new_spec = lambda sec: pl.BlockSpec((None, ts, SECTION), lambda b, p, pt: (b, 0, sec))

    def page_spec(r):
        return pl.BlockSpec((None, page_rows, DV_DIFF), lambda b, p, pt: (pt[b, p * pages + r], 0, 0))

    grid_spec = pltpu.PrefetchScalarGridSpec(
        num_scalar_prefetch=1,
        grid=(batch, n_pages // pages),
        in_specs=[
            pl.BlockSpec(memory_space=pltpu.SMEM),
            pl.BlockSpec((4, D_HEAD), lambda b, p, pt: (0, 0)),
            new_spec(3), new_spec(4), new_spec(5),
            pl.BlockSpec((1, DV_DIFF), lambda b, p, pt: (0, 0)),
        ] + [page_spec(r) for r in range(pages)] + [page_spec(r) for r in range(pages)],
        out_specs=pl.BlockSpec((None, ts, width), lambda b, p, pt: (b, 0, 0)),
        scratch_shapes=[
            pltpu.VMEM((rows, LANES), F32),
            pltpu.VMEM((rows, page_rows), F32),
            pltpu.VMEM((rows, 1), F32),
            pltpu.VMEM((rows, 1), F32),
            pltpu.VMEM((rows, DV_DIFF), F32),
        ],
    )
    y = pl.pallas_call(
        functools.partial(_diff_sample_kernel, ts=ts, lam_init=lam_init),
        grid_spec=grid_spec,
        out_shape=jax.ShapeDtypeStruct((batch, ts, width), BF16),
        compiler_params=_params(("parallel", "arbitrary")),
        name="diff_attention_sample",
    )(page_table, logit_bound, lam_vecs, z3, z3, z3, gn, *([cache_k] * pages), *([cache_v] * pages))
    return y.reshape(batch * ts, width)


def _prompt_schedule(batch, seq, tq, n_steps):
    nq = seq // tq
    budget = n_steps // (batch * H_DIFF)
    per_tile = {qi: ATTN_SLOTS for qi in range(nq)}
    spare = budget - sum(-(-(qi + 1) // ATTN_SLOTS) for qi in range(nq))
    if spare < 0:
        return None
    for qi in reversed(range(nq)):
        extra = -(-(qi + 1) // (ATTN_SLOTS - 1)) - -(-(qi + 1) // ATTN_SLOTS)
        if 0 < extra <= spare:
            per_tile[qi] = ATTN_SLOTS - 1
            spare -= extra
    rec = []
    for bp in range(batch):
        for h in range(H_DIFF):
            for qi in range(nq):
                nblk = qi + 1
                size = per_tile[qi]
                groups = -(-nblk // size)
                for g in range(groups):
                    kbs = [g * size + j for j in range(ATTN_SLOTS)]
                    used = min(size, nblk - g * size)
                    rec.append([bp * nq + qi, h, bp]
                               + [min(kb, nblk - 1) for kb in kbs]
                               + [(qi - kb) * tq if j < used else NEVER for j, kb in enumerate(kbs)]
                               + [int(g == 0), int(g == groups - 1), used])
    idle = rec[-1][:6] + [NEVER] * ATTN_SLOTS + [0, 0, 0]
    rec += [idle] * (n_steps - len(rec))
    return np.asarray(rec, np.int32).reshape(-1)


def _fused_attn_kernel(pt_ref, sched_ref, lam_ref, q_ref, kn_ref, vn_ref, gn_ref, pq_ref, pk_ref, pv_ref,
                       *rest, ts, lam_init):
    del pt_ref
    pages = PAGES_PER_STEP
    k_refs = rest[:pages]
    v_refs = rest[pages:2 * pages]
    y_ref, yp_ref, q_scr, bias_scr, l_ref, acc_ref, rel_scr, pacc_ref = rest[2 * pages:]
    p_step = pl.program_id(1)
    step = pl.program_id(0) * pl.num_programs(1) + p_step
    hr = 2 * ts
    rows = H_DIFF * hr
    tq = pq_ref.shape[0]

    def head_cols(x, h):
        return x[:, h * DV_DIFF:(h + 1) * DV_DIFF]

    @pl.when(step == 0)
    def _():
        rel_scr[...] = (lax.broadcasted_iota(jnp.int32, rel_scr.shape, 1)
                        - lax.broadcasted_iota(jnp.int32, rel_scr.shape, 0) % tq)
        pacc_ref[...] = jnp.zeros(pacc_ref.shape, F32)

    @pl.when(p_step == 0)
    def _():
        q32 = q_ref[...].astype(F32)
        r8 = lax.broadcasted_iota(jnp.int32, (hr, LANES), 0)
        lane8 = lax.broadcasted_iota(jnp.int32, (hr, LANES), 1)
        keep = (lane8 // D_HEAD) == (r8 // ts)
        q_heads = []
        for h in range(H_DIFF):
            blk = head_cols(q32, h)
            qt = jnp.zeros((hr, LANES), F32)
            for t in range(ts):
                qt = jnp.where(r8 % ts == t, blk[t:t + 1, :], qt)
            q_heads.append(jnp.where(keep, qt, 0.0))
        q_all = jnp.concatenate(q_heads, axis=0)
        q_scr[...] = q_all
        brow = lax.broadcasted_iota(jnp.int32, bias_scr.shape, 0)
        bcol = lax.broadcasted_iota(jnp.int32, bias_scr.shape, 1)
        bias_scr[...] = jnp.where(bcol % H_DIFF == brow // hr, 0.0, -jnp.inf)

        kn = kn_ref[...].astype(F32)
        vn = vn_ref[...].astype(F32)

        def per_row(x, t):
            return jnp.concatenate(
                [jnp.broadcast_to(head_cols(x, h)[t:t + 1, :], (hr, LANES)) for h in range(H_DIFF)], axis=0)

        qpos = lax.broadcasted_iota(jnp.int32, (rows, 1), 0) % ts
        l0 = jnp.zeros((rows, 1), F32)
        a0 = jnp.zeros((rows, LANES), F32)
        for t in range(ts):
            s_new = jnp.sum(q_all * per_row(kn, t), axis=-1, keepdims=True)
            pt = jnp.where(qpos >= t, jnp.exp2(s_new), 0.0)
            l0 = l0 + pt
            a0 = a0 + pt * per_row(vn, t)
        l_ref[...] = l0
        acc_ref[...] = a0

    qb = q_scr[...].astype(BF16)
    bias = bias_scr[...]
    cols = bias.shape[1]
    pv = jnp.zeros((rows, DV_DIFF), F32)
    psum = jnp.zeros((rows, LANES), F32)
    for r in range(pages):
        s = lax.dot_general(qb, k_refs[r][...].astype(BF16), NT_DIMS, preferred_element_type=F32) + bias
        p = jnp.exp2(s)
        for c0 in range(0, cols, LANES):
            psum = psum + p[:, c0:c0 + LANES]
        pv = pv + jnp.dot(p.astype(BF16), v_refs[r][...].astype(BF16), preferred_element_type=F32)
    l_ref[...] += jnp.sum(psum, axis=-1, keepdims=True)
    acc_ref[...] += pv

    rec = step * SCHED_REC
    used = sched_ref[rec + 5 + 2 * ATTN_SLOTS]

    def prompt_blocks(n):
        q = pq_ref[...]
        lane = lax.broadcasted_iota(jnp.int32, q.shape, 1)
        zero = jnp.zeros_like(q)
        qs = jnp.concatenate([jnp.where(lane < D_HEAD, q, zero), jnp.where(lane >= D_HEAD, q, zero)], axis=0)
        ones = jnp.ones((tq, LANES), BF16)
        rel = rel_scr[...]
        pacc = jnp.where(sched_ref[rec + 3 + 2 * ATTN_SLOTS] == 1, 0.0, pacc_ref[...])
        for j in range(n):
            off = pl.multiple_of(sched_ref[rec + 3 + j] * tq, tq)
            visible = rel <= sched_ref[rec + 3 + ATTN_SLOTS + j]
            s = lax.dot_general(qs, pk_ref[pl.ds(off, tq), :], NT_DIMS, preferred_element_type=F32)
            p = jnp.exp2(jnp.where(visible, s, -jnp.inf)).astype(BF16)
            pacc = pacc + jnp.dot(p, jnp.concatenate([pv_ref[pl.ds(off, tq), :], ones], axis=1),
                                  preferred_element_type=F32)
        pacc_ref[...] = pacc

    for n in range(1, ATTN_SLOTS + 1):
        pl.when(used == n)(functools.partial(prompt_blocks, n))

    @pl.when(p_step == pl.num_programs(1) - 1)
    def _():
        on = acc_ref[...] / l_ref[...]
        lam = _lambda_value(lam_ref, lam_init)
        gain = gn_ref[...]
        for h in range(H_DIFF):
            o = on[h * hr:h * hr + ts] - lam * on[h * hr + ts:(h + 1) * hr]
            y_ref[:, h * DV_DIFF:(h + 1) * DV_DIFF] = _head_norm(o, gain, lam_init).astype(BF16)

    @pl.when(sched_ref[rec + 4 + 2 * ATTN_SLOTS] == 1)
    def _():
        acc = pacc_ref[...]
        on = acc[:, :DV_DIFF] / acc[:, DV_DIFF:]
        lam = _lambda_value(lam_ref, lam_init)
        o = on[:tq] - lam * on[tq:]
        yp_ref[...] = _head_norm(o, gn_ref[...], lam_init).astype(BF16)


def _fused_attention(z16_s, z16_p, cache_k, cache_v, page_table, sched, lam_vecs, gn, batch_s, ts, batch_p, seq,
                     lam_init):
    n_pages = page_table.shape[1]
    page_rows = cache_k.shape[1]
    width = H_DIFF * DV_DIFF
    pages = PAGES_PER_STEP
    n_steps = n_pages // pages
    rows = 2 * H_DIFF * ts
    tq = min(ATTN_TILE, seq)
    q_col = 3 * SECTION // LANES
    k_col = 4 * SECTION // LANES
    v_col = 5 * SECTION // LANES
    z3 = z16_s.reshape(batch_s, ts, N_SECTIONS * SECTION)
    new_spec = lambda sec: pl.BlockSpec((None, ts, SECTION), lambda b, p, pt, sc: (b, 0, sec))

    def field(b, p, sc, i):
        return sc[(b * n_steps + p) * SCHED_REC + i]

    def page_spec(r):
        return pl.BlockSpec((None, page_rows, DV_DIFF), lambda b, p, pt, sc: (pt[b, p * pages + r], 0, 0))

    grid_spec = pltpu.PrefetchScalarGridSpec(
        num_scalar_prefetch=2,
        grid=(batch_s, n_steps),
        in_specs=[
            pl.BlockSpec((4, D_HEAD), lambda b, p, pt, sc: (0, 0)),
            new_spec(3), new_spec(4), new_spec(5),
            pl.BlockSpec((1, DV_DIFF), lambda b, p, pt, sc: (0, 0)),
            pl.BlockSpec((tq, LANES), lambda b, p, pt, sc: (field(b, p, sc, 0), q_col + field(b, p, sc, 1))),
            pl.BlockSpec((seq, LANES), lambda b, p, pt, sc: (field(b, p, sc, 2), k_col + field(b, p, sc, 1))),
            pl.BlockSpec((seq, LANES), lambda b, p, pt, sc: (field(b, p, sc, 2), v_col + field(b, p, sc, 1))),
        ] + [page_spec(r) for r in range(pages)] + [page_spec(r) for r in range(pages)],
        out_specs=[
            pl.BlockSpec((None, ts, width), lambda b, p, pt, sc: (b, 0, 0)),
            pl.BlockSpec((tq, DV_DIFF), lambda b, p, pt, sc: (field(b, p, sc, 0), field(b, p, sc, 1))),
        ],
        scratch_shapes=[
            pltpu.VMEM((rows, LANES), F32),
            pltpu.VMEM((rows, page_rows), F32),
            pltpu.VMEM((rows, 1), F32),
            pltpu.VMEM((rows, DV_DIFF), F32),
            pltpu.VMEM((2 * tq, tq), jnp.int32),
            pltpu.VMEM((2 * tq, 2 * DV_DIFF), F32),
        ],
    )
    y_s, y_p = pl.pallas_call(
        functools.partial(_fused_attn_kernel, ts=ts, lam_init=lam_init),
        grid_spec=grid_spec,
        out_shape=[jax.ShapeDtypeStruct((batch_s, ts, width), BF16),
                   jax.ShapeDtypeStruct((batch_p * seq, width), BF16)],
        compiler_params=_params(("arbitrary", "arbitrary")),
        name="fused_attention",
    )(page_table, sched, lam_vecs, z3, z3, z3, gn, z16_p, z16_p, z16_p,
      *([cache_k] * pages), *([cache_v] * pages))
    return y_s.reshape(batch_s * ts, width), y_p


def _merge_kernel(x_ref, yr_ref, yd_ref, gr_ref, gd_ref, wr_ref, wd_ref, wo_ref, g_ref, x1_ref, h_ref):
    a = jnp.dot(yr_ref[...], wr_ref[...], preferred_element_type=F32)
    b = jnp.dot(yd_ref[...], wd_ref[...], preferred_element_type=F32)
    m = gr_ref[...].astype(F32) * a + gd_ref[...].astype(F32) * b
    x1 = x_ref[...] + jnp.dot(m.astype(BF16), wo_ref[...], preferred_element_type=F32)
    x1_ref[...] = x1
    ms = jnp.mean(x1 * x1, axis=-1, keepdims=True)
    h_ref[...] = (x1 * lax.rsqrt(ms + EPS) * g_ref[...]).astype(BF16)


def _merge(x, y_r, y_d, z16, w_br_ret, w_br_diff, w_out, g_ffn):
    n = x.shape[0]
    tm = min(TOKEN_TILE, n)
    tile = lambda col: pl.BlockSpec((tm, D_MODEL), lambda i: (i, col))
    weight = pl.BlockSpec((D_MODEL, D_MODEL), lambda i: (0, 0))
    return pl.pallas_call(
        _merge_kernel,
        grid=(n // tm,),
        in_specs=[tile(0), tile(0), tile(0), tile(6), tile(7), weight, weight, weight,
                  pl.BlockSpec((1, D_MODEL), lambda i: (0, 0))],
        out_specs=[tile(0), tile(0)],
        out_shape=[jax.ShapeDtypeStruct((n, D_MODEL), F32), jax.ShapeDtypeStruct((n, D_MODEL), BF16)],
        compiler_params=_params(("parallel",)),
        name="merge",
    )(x, y_r, y_d, z16, z16, w_br_ret, w_br_diff, w_out, g_ffn)


def _ffn_chunks(d_ff):
    edges = list(range(0, d_ff, SECTION)) + [d_ff]
    return list(zip(edges[:-1], edges[1:]))


def _ffn_kernel(x_ref, h_ref, wg_ref, wu_ref, wd_ref, o_ref):
    h = h_ref[...]
    acc = x_ref[...]
    for c0, c1 in _ffn_chunks(wg_ref.shape[1]):
        g = jnp.dot(h, wg_ref[:, c0:c1], preferred_element_type=F32)
        u = jnp.dot(h, wu_ref[:, c0:c1], preferred_element_type=F32)
        a = (g * _sigmoid(g) * u).astype(BF16)
        acc = acc + jnp.dot(a, wd_ref[c0:c1, :], preferred_element_type=F32)
    o_ref[...] = acc


def _ffn(x1, h, wg, wu, wd):
    n = x1.shape[0]
    d_ff = wg.shape[1]
    tm = min(TOKEN_TILE, n)
    tile = pl.BlockSpec((tm, D_MODEL), lambda i: (i, 0))
    single = pl.Buffered(1)
    return pl.pallas_call(
        _ffn_kernel,
        grid=(n // tm,),
        in_specs=[tile, tile,
                  pl.BlockSpec((D_MODEL, d_ff), lambda i: (0, 0), pipeline_mode=single),
                  pl.BlockSpec((D_MODEL, d_ff), lambda i: (0, 0), pipeline_mode=single),
                  pl.BlockSpec((d_ff, D_MODEL), lambda i: (0, 0), pipeline_mode=single)],
        out_specs=tile,
        out_shape=jax.ShapeDtypeStruct((n, D_MODEL), F32),
        compiler_params=_params(("parallel",)),
        name="ffn",
    )(x1, h, wg, wu, wd)


def _rope_tables(pos, half):
    inv = 1.0 / jnp.power(ROPE_THETA, jnp.arange(half, dtype=F32) / half)
    ang = pos.astype(F32)[:, None] * inv[None, :]
    cos = jnp.cos(ang)
    sin = jnp.sin(ang)
    reps = LANES // (2 * half)
    return (jnp.tile(jnp.concatenate([cos, cos], axis=-1), (1, reps)),
            jnp.tile(jnp.concatenate([-sin, sin], axis=-1), (1, reps)))


def _layer_tables(pos):
    cr, sr = _rope_tables(pos, DK_RET // 2)
    cd, sd = _rope_tables(pos, D_HEAD // 2)
    return (cr, sr, cd, sd)


def kernel(x_prompt, x_sample, cache_k, cache_v, state_ret, page_table, norm_mix_g, w_in, ret_gn_g, diff_qnorm_g, diff_knorm_g, lambda_q1, lambda_k1, lambda_q2, lambda_k2, diff_gn_g, w_br_ret, w_br_diff, w_out, norm_ffn_g, w_ffn_gate, w_ffn_up, w_ffn_down):
    B, T, D = x_prompt.shape
    Bs, Ts, _ = x_sample.shape
    depth, n_phys, page = cache_k.shape[:3]
    past_len = page_table.shape[1] * page

    tabs_p = _layer_tables(jnp.arange(T, dtype=jnp.int32))
    tabs_s = _layer_tables(jnp.tile(past_len + jnp.arange(Ts, dtype=jnp.int32), Bs))
    lg = jnp.log(1.0 - jnp.power(2.0, -5.0 - jnp.arange(H_RET, dtype=F32)))
    lg_tab = jnp.broadcast_to(lg[:, None, None], (H_RET, 1, LANES))
    gidx = jnp.arange(NORM_GROUP_TILE) // D_HEAD
    bd = (gidx[:, None] == gidx[None, :]).astype(BF16)
    ck = cache_k.reshape(depth * n_phys, page * H_DIFF, DV_DIFF)
    cv = cache_v.reshape(depth * n_phys, page * H_DIFF, DV_DIFF)

    sched = None
    if min(ATTN_TILE, T) == min(ATTN_KV_TILE, T) and page_table.shape[1] % PAGES_PER_STEP == 0:
        sched = _prompt_schedule(B, T, min(ATTN_TILE, T), Bs * (page_table.shape[1] // PAGES_PER_STEP))
    xp = x_prompt.reshape(B * T, D)
    xs = x_sample.reshape(Bs * Ts, D)
    outs = [[] for _ in range(6)]
    for l in range(depth):
        lam_init = 0.8 - 0.6 * math.exp(-0.3 * l)
        w_in_l = w_in[l].astype(BF16)
        g_mix = norm_mix_g[l].reshape(1, D)
        g_ffn = norm_ffn_g[l].reshape(1, D)
        qg = jnp.tile(diff_qnorm_g[l], LANES // D_HEAD).reshape(1, LANES)
        kg = jnp.tile(diff_knorm_g[l], LANES // D_HEAD).reshape(1, LANES)
        gn_r = ret_gn_g[l].reshape(H_RET, 1, DV_RET)
        gn_d = diff_gn_g[l].reshape(1, DV_DIFF)
        logit_bound = (1.02 * LOG2E * D_HEAD ** 0.5 * jnp.max(jnp.abs(diff_qnorm_g[l]))
                       * jnp.max(jnp.abs(diff_knorm_g[l]))).astype(F32).reshape(1)
        lam_vecs = jnp.stack([lambda_q1[l], lambda_k1[l], lambda_q2[l], lambda_k2[l]]).astype(F32)
        wr = w_br_ret[l].astype(BF16)
        wdf = w_br_diff[l].astype(BF16)
        wo = w_out[l].astype(BF16)
        wg = w_ffn_gate[l].astype(BF16)
        wu = w_ffn_up[l].astype(BF16)
        wd = w_ffn_down[l].astype(BF16)

        zp, k32_p, v32_p = _in_projection(xp, g_mix, w_in_l, tabs_p, qg, kg, bd, T)
        zs, k32_s, v32_s = _in_projection(xs, g_mix, w_in_l, tabs_s, qg, kg, bd, Bs * Ts)
        yr_p, s_p = _retention_prompt(zp, lg_tab, gn_r, jnp.zeros((B, H_RET, DK_RET, DV_RET), F32), B, T)
        yr_s, s_s = _retention_sample(zs, lg_tab, gn_r, state_ret[l], Bs, Ts)
        pt_l = page_table + l * n_phys

        def separate_attention():
            return (_diff_sample(zs, ck, cv, pt_l, logit_bound, lam_vecs, gn_d, Bs, Ts, lam_init),
                    _diff_prompt(zp, logit_bound, lam_vecs, gn_d, B, T, lam_init))

        def fused_attention():
            return _fused_attention(zs, zp, ck, cv, pt_l, jnp.asarray(sched), lam_vecs, gn_d, Bs, Ts, B, T, lam_init)

        if sched is None:
            yd_s, yd_p = separate_attention()
        else:
            yd_s, yd_p = lax.cond(logit_bound[0] <= SAFE_LOGIT_BOUND, fused_attention, separate_attention)

        x1, h2 = _merge(xp, yr_p, yd_p, zp, wr, wdf, wo, g_ffn)
        xp = _ffn(x1, h2, wg, wu, wd)
        x1, h2 = _merge(xs, yr_s, yd_s, zs, wr, wdf, wo, g_ffn)
        xs = _ffn(x1, h2, wg, wu, wd)
        outs[0].append(k32_p.reshape(B, T, H_DIFF, 2 * D_HEAD))
        outs[1].append(v32_p.reshape(B, T, H_DIFF, DV_DIFF))
        outs[2].append(s_p)
        outs[3].append(k32_s.reshape(Bs, Ts, H_DIFF, 2 * D_HEAD))
        outs[4].append(v32_s.reshape(Bs, Ts, H_DIFF, DV_DIFF))
        outs[5].append(s_s)

    stacked = [jnp.stack(o) for o in outs]
    return (xp.reshape(B, T, D), xs.reshape(Bs, Ts, D),
            stacked[0], stacked[1], stacked[2], stacked[3], stacked[4], stacked[5])
```
